```python
import math
import jax, jax.numpy as jnp
from jax import lax
import numpy as np

D_MODEL = 1024
BATCH = 4
SEQ = 4096
DEPTH = 1

CTX_LEN = 256
GRID_W = 64
D_MIX = D_MODEL
RG_W = D_MIX // 2
RG_HEADS = 8
RG_HEAD_DIM = RG_W // RG_HEADS
RG_CONV = 4
RG_C = 8.0
S5_W = D_MIX - RG_W
S5_GROUP = 16
S5_GROUPS = S5_W // S5_GROUP
S5_STATE = 64
PEER_HEADS = 8
PEER_TOPK = 16
PEER_NKEYS = 128
PEER_EXPERTS = PEER_NKEYS * PEER_NKEYS
PEER_QDIM = 256
PEER_HALF = PEER_QDIM // 2
PEER_CHUNK = 128
N_MOD = 6
EPS = 1e-6

kernel_name = "hymba_rglru_s5_peer_diffusion_block"


def _rms(x, g):
    x32 = x.astype(jnp.float32)
    return x32 * lax.rsqrt(jnp.mean(x32 * x32, axis=-1, keepdims=True) + EPS) * g.astype(jnp.float32)


def _modulation(cvec, w_mod, b_mod):
    m = jax.nn.silu(cvec.astype(jnp.float32)) @ w_mod + b_mod
    return jnp.split(m, N_MOD, axis=-1)


def _dwconv_centred(x, w, b):
    L = x.shape[1]
    left = RG_CONV // 2
    xp = jnp.pad(x, ((0, 0), (left, RG_CONV - 1 - left), (0, 0)))
    return sum(xp[:, k:k + L] * w[k] for k in range(RG_CONV)) + b


def _combine_real(e1, e2):
    a1, b1 = e1
    a2, b2 = e2
    return a1 * a2, a2 * b1 + b2


def _scan_real(a, b, h0, reverse):
    if reverse:
        a = jnp.flip(a, 1)
        b = jnp.flip(b, 1)
    b = b.at[:, 0].add(a[:, 0] * h0)
    _, h = lax.associative_scan(_combine_real, (a, b), axis=1)
    h_last = h[:, -1]
    if reverse:
        h = jnp.flip(h, 1)
    return h, h_last


def _rglru_coeffs(xc, wa, ba, wx, bx, lam):
    B, L, _ = xc.shape
    xh = xc.reshape(B, L, RG_HEADS, RG_HEAD_DIM)
    r = jax.nn.sigmoid(jnp.einsum('blhi,hij->blhj', xh, wa).reshape(B, L, RG_W) + ba)
    i = jax.nn.sigmoid(jnp.einsum('blhi,hij->blhj', xh, wx).reshape(B, L, RG_W) + bx)
    log_a = -RG_C * r * jax.nn.softplus(-lam)
    a = jnp.exp(log_a)
    b = jnp.sqrt(jnp.maximum(-jnp.expm1(2.0 * log_a), 0.0)) * (i * xc)
    return a, b


def _rglru_branch(xr_c, xr_l, conv_w, conv_b, wa, ba, wx, bx, lam):
    xc_c = _dwconv_centred(xr_c, conv_w, conv_b)
    xc_l = _dwconv_centred(xr_l, conv_w, conv_b)
    outs_c, outs_l = [], []
    for d, rev in enumerate((False, True)):
        a, b = _rglru_coeffs(xc_c, wa[d], ba[d], wx[d], bx[d], lam[d])
        h_c, h_fin = _scan_real(a, b, jnp.zeros_like(a[:, 0]), rev)
        a, b = _rglru_coeffs(xc_l, wa[d], ba[d], wx[d], bx[d], lam[d])
        h_l, _ = _scan_real(a, b, h_fin, rev)
        outs_c.append(h_c)
        outs_l.append(h_l)
    return outs_c[0] + outs_c[1], outs_l[0] + outs_l[1]


def _s5_discretise(lam_re, lam_im, log_step, b_re, b_im):
    lre = jnp.minimum(lam_re, -1e-4)
    step = jnp.exp(log_step)[:, None]
    mag = jnp.exp(lre * step)
    ar = mag * jnp.cos(lam_im * step)
    ai = mag * jnp.sin(lam_im * step)
    nr, ni = ar - 1.0, ai
    den = lre * lre + lam_im * lam_im
    cr = (nr * lre + ni * lam_im) / den
    ci = (ni * lre - nr * lam_im) / den
    bbr = cr[..., None] * b_re - ci[..., None] * b_im
    bbi = cr[..., None] * b_im + ci[..., None] * b_re
    return ar, ai, bbr, bbi


def _combine_complex(e1, e2):
    ar1, ai1, br1, bi1 = e1
    ar2, ai2, br2, bi2 = e2
    return (ar1 * ar2 - ai1 * ai2, ar1 * ai2 + ai1 * ar2,
            ar2 * br1 - ai2 * bi1 + br2, ar2 * bi1 + ai2 * br1 + bi2)


def _s5_dir(u, h0r, h0i, ar, ai, bbr, bbi, c_re, c_im, reverse):
    L = u.shape[0]
    bur = jnp.einsum('lbgh,gph->lbgp', u, bbr)
    bui = jnp.einsum('lbgh,gph->lbgp', u, bbi)
    if reverse:
        bur = jnp.flip(bur, 0)
        bui = jnp.flip(bui, 0)
    bur = bur.at[0].add(ar * h0r - ai * h0i)
    bui = bui.at[0].add(ar * h0i + ai * h0r)
    art = jnp.broadcast_to(ar[None, None], (L, 1) + ar.shape)
    ait = jnp.broadcast_to(ai[None, None], (L, 1) + ai.shape)
    _, _, hr, hi = lax.associative_scan(_combine_complex, (art, ait, bur, bui), axis=0)
    fr, fi = hr[-1], hi[-1]
    if reverse:
        hr = jnp.flip(hr, 0)
        hi = jnp.flip(hi, 0)
    y = jnp.einsum('lbgp,ghp->lbgh', hr, c_re) - jnp.einsum('lbgp,ghp->lbgh', hi, c_im)
    return y, fr, fi


def _s5_branch(u_c, u_l, lam_re, lam_im, log_step, b_re, b_im, c_re, c_im, d_skip):
    def to_lbgh(u):
        B, L, _ = u.shape
        return u.astype(jnp.float32).reshape(B, L, S5_GROUPS, S5_GROUP).transpose(1, 0, 2, 3)

    def from_lbgh(y):
        L, B = y.shape[0], y.shape[1]
        return y.transpose(1, 0, 2, 3).reshape(B, L, S5_W)

    uc4, ul4 = to_lbgh(u_c), to_lbgh(u_l)
    zeros = jnp.zeros((u_c.shape[0], S5_GROUPS, S5_STATE), jnp.float32)
    outs_c, outs_l = [], []
    for d, rev in enumerate((False, True)):
        ar, ai, bbr, bbi = _s5_discretise(lam_re[d], lam_im[d], log_step[d], b_re[d], b_im[d])
        y_c, fr, fi = _s5_dir(uc4, zeros, zeros, ar, ai, bbr, bbi, c_re[d], c_im[d], rev)
        y_l, _, _ = _s5_dir(ul4, fr, fi, ar, ai, bbr, bbi, c_re[d], c_im[d], rev)
        outs_c.append(y_c)
        outs_l.append(y_l)
    y_c = from_lbgh(outs_c[0] + outs_c[1]) + d_skip * u_c
    y_l = from_lbgh(outs_l[0] + outs_l[1]) + d_skip * u_l
    return y_c, y_l


def _glu(y, w, b):
    z = jax.nn.gelu(y)
    return z * jax.nn.sigmoid(z @ w + b)


def _to_colmajor(z, rows):
    B, L, C = z.shape
    return z.reshape(B, rows, GRID_W, C).transpose(0, 2, 1, 3).reshape(B, L, C)


def _from_colmajor(z, rows):
    B, L, C = z.shape
    return z.reshape(B, GRID_W, rows, C).transpose(0, 2, 1, 3).reshape(B, L, C)


def _mixer(h_c, h_l, rows, w_in, conv_w, conv_b, rg_wa, rg_ba, rg_wx, rg_bx, rg_lam,
           lam_re, lam_im, log_step, b_re, b_im, c_re, c_im, d_skip, w_glu, b_glu, w_out, need_ctx):
    p_c = h_c @ w_in
    p_l = h_l @ w_in
    xr_c, gr_c, us_c = p_c[..., :RG_W], p_c[..., RG_W:2 * RG_W], p_c[..., 2 * RG_W:]
    xr_l, gr_l, us_l = p_l[..., :RG_W], p_l[..., RG_W:2 * RG_W], p_l[..., 2 * RG_W:]
    rg_c, rg_l = _rglru_branch(xr_c, xr_l, conv_w, conv_b, rg_wa, rg_ba, rg_wx, rg_bx, rg_lam)
    s5_c, s5_l_cm = _s5_branch(us_c, _to_colmajor(us_l, rows), lam_re, lam_im, log_step,
                               b_re, b_im, c_re, c_im, d_skip)
    s5_l = _from_colmajor(s5_l_cm, rows)
    out_l = jnp.concatenate([rg_l * jax.nn.gelu(gr_l), _glu(s5_l, w_glu, b_glu)], axis=-1) @ w_out
    out_c = None
    if need_ctx:
        out_c = jnp.concatenate([rg_c * jax.nn.gelu(gr_c), _glu(s5_c, w_glu, b_glu)], axis=-1) @ w_out
    return out_c, out_l


def _peer(h, w_q, keys, u_tab, v_tab):
    B, L, D = h.shape
    T = B * L
    t = h.reshape(T, D)
    q = (t @ w_q).reshape(T, PEER_HEADS, 2, PEER_HALF)
    s_a = jnp.einsum('thd,hkd->thk', q[:, :, 0], keys[:, 0])
    s_b = jnp.einsum('thd,hkd->thk', q[:, :, 1], keys[:, 1])
    va, ia = lax.top_k(s_a, PEER_TOPK)
    vb, ib = lax.top_k(s_b, PEER_TOPK)
    cand = (va[..., :, None] + vb[..., None, :]).reshape(T, PEER_HEADS, PEER_TOPK * PEER_TOPK)
    vs, js = lax.top_k(cand, PEER_TOPK)
    ea = jnp.take_along_axis(ia, js // PEER_TOPK, axis=-1)
    eb = jnp.take_along_axis(ib, js % PEER_TOPK, axis=-1)
    idx = (ea * PEER_NKEYS + eb).reshape(T, PEER_HEADS * PEER_TOPK)
    g = jax.nn.softmax(vs.astype(jnp.float32), axis=-1).reshape(T, PEER_HEADS * PEER_TOPK)
    n_blk = T // PEER_CHUNK
    HK = PEER_HEADS * PEER_TOPK

    def one_block(args):
        tc, ic, gc = args
        u_sel = jnp.take(u_tab, ic, axis=0)
        act = jax.nn.gelu(jnp.einsum('cd,ckd->ck', tc, u_sel))
        v_sel = jnp.take(v_tab, ic, axis=0)
        return jnp.einsum('ck,ckd->cd', gc * act, v_sel)

    y = lax.map(one_block, (t.reshape(n_blk, PEER_CHUNK, D),
                            idx.reshape(n_blk, PEER_CHUNK, HK),
                            g.reshape(n_blk, PEER_CHUNK, HK)))
    return y.reshape(B, L, D)


def setup_inputs(seed: int = 0) -> dict:
    key = jax.random.key(seed)
    ks = jax.random.split(key, 40)
    f32 = jnp.float32

    def nrm(k, shape, scale):
        return jax.random.normal(k, shape, f32) * scale

    NL = DEPTH
    x = nrm(ks[0], (BATCH, SEQ, D_MODEL), 1.0)
    c = nrm(ks[1], (BATCH, D_MODEL), 1.0)
    ctx = nrm(ks[2], (BATCH, CTX_LEN, D_MODEL), 1.0)
    c_ctx = nrm(ks[3], (D_MODEL,), 1.0)
    w_mod = nrm(ks[4], (NL, D_MODEL, N_MOD * D_MODEL), 0.5 * D_MODEL ** -0.5)
    b_mod = nrm(ks[5], (NL, N_MOD * D_MODEL), 0.02)
    g_pre_mix = 1.0 + nrm(ks[6], (NL, D_MODEL), 0.05)
    g_post_mix = 1.0 + nrm(ks[7], (NL, D_MODEL), 0.05)
    g_pre_ffn = 1.0 + nrm(ks[8], (NL, D_MODEL), 0.05)
    g_post_ffn = 1.0 + nrm(ks[9], (NL, D_MODEL), 0.05)
    w_in = nrm(ks[10], (NL, D_MODEL, 2 * RG_W + S5_W), D_MODEL ** -0.5)
    rg_conv_w = nrm(ks[11], (NL, RG_CONV, RG_W), RG_CONV ** -0.5)
    rg_conv_b = nrm(ks[12], (NL, RG_W), 0.02)
    rg_wa = nrm(ks[13], (NL, 2, RG_HEADS, RG_HEAD_DIM, RG_HEAD_DIM), RG_HEAD_DIM ** -0.5)
    rg_ba = nrm(ks[14], (NL, 2, RG_W), 0.02)
    rg_wx = nrm(ks[15], (NL, 2, RG_HEADS, RG_HEAD_DIM, RG_HEAD_DIM), RG_HEAD_DIM ** -0.5)
    rg_bx = nrm(ks[16], (NL, 2, RG_W), 0.02)
    a_init = jax.random.uniform(ks[17], (NL, 2, RG_W), f32, minval=0.9, maxval=0.999)
    s = a_init ** (1.0 / RG_C)
    rg_lam = jnp.log(s) - jnp.log1p(-s)
    n_idx = jnp.arange(S5_STATE, dtype=f32)
    s5_lam_re = -0.5 + nrm(ks[18], (NL, 2, S5_GROUPS, S5_STATE), 0.01)
    s5_lam_im = math.pi * n_idx + nrm(ks[19], (NL, 2, S5_GROUPS, S5_STATE), 0.01)
    s5_log_step = jax.random.uniform(ks[20], (NL, 2, S5_GROUPS), f32,
                                     minval=math.log(1e-3), maxval=math.log(1e-1))
    s5_b_re = nrm(ks[21], (NL, 2, S5_GROUPS, S5_STATE, S5_GROUP), (2.0 * S5_GROUP) ** -0.5)
    s5_b_im = nrm(ks[22], (NL, 2, S5_GROUPS, S5_STATE, S5_GROUP), (2.0 * S5_GROUP) ** -0.5)
    s5_c_re = nrm(ks[23], (NL, 2, S5_GROUPS, S5_GROUP, S5_STATE), (2.0 * S5_STATE) ** -0.5)
    s5_c_im = nrm(ks[24], (NL, 2, S5_GROUPS, S5_GROUP, S5_STATE), (2.0 * S5_STATE) ** -0.5)
    s5_d = nrm(ks[25], (NL, S5_W), 1.0)
    s5_w_glu = nrm(ks[26], (NL, S5_W, S5_W), S5_W ** -0.5)
    s5_b_glu = nrm(ks[27], (NL, S5_W), 0.02)
    w_out = nrm(ks[28], (NL, D_MIX, D_MODEL), D_MIX ** -0.5)
    peer_wq = nrm(ks[29], (NL, D_MODEL, PEER_HEADS * PEER_QDIM), D_MODEL ** -0.5)
    peer_keys = nrm(ks[30], (NL, PEER_HEADS, 2, PEER_NKEYS, PEER_HALF), PEER_HALF ** -0.5)
    peer_u = nrm(ks[31], (NL, PEER_EXPERTS, D_MODEL), D_MODEL ** -0.5)
    peer_v = nrm(ks[32], (NL, PEER_EXPERTS, D_MODEL), D_MODEL ** -0.5)
    return {"x": x, "c": c, "ctx": ctx, "c_ctx": c_ctx, "w_mod": w_mod, "b_mod": b_mod,
            "g_pre_mix": g_pre_mix, "g_post_mix": g_post_mix, "g_pre_ffn": g_pre_ffn,
            "g_post_ffn": g_post_ffn, "w_in": w_in, "rg_conv_w": rg_conv_w, "rg_conv_b": rg_conv_b,
            "rg_wa": rg_wa, "rg_ba": rg_ba, "rg_wx": rg_wx, "rg_bx": rg_bx, "rg_lam": rg_lam,
            "s5_lam_re": s5_lam_re, "s5_lam_im": s5_lam_im, "s5_log_step": s5_log_step,
            "s5_b_re": s5_b_re, "s5_b_im": s5_b_im, "s5_c_re": s5_c_re, "s5_c_im": s5_c_im,
            "s5_d": s5_d, "s5_w_glu": s5_w_glu, "s5_b_glu": s5_b_glu, "w_out": w_out,
            "peer_wq": peer_wq, "peer_keys": peer_keys, "peer_u": peer_u, "peer_v": peer_v}


def reference(x, c, ctx, c_ctx, w_mod, b_mod, g_pre_mix, g_post_mix, g_pre_ffn, g_post_ffn,
              w_in, rg_conv_w, rg_conv_b, rg_wa, rg_ba, rg_wx, rg_bx, rg_lam,
              s5_lam_re, s5_lam_im, s5_log_step, s5_b_re, s5_b_im, s5_c_re, s5_c_im,
              s5_d, s5_w_glu, s5_b_glu, w_out, peer_wq, peer_keys, peer_u, peer_v):
    n_lat = x.shape[1]
    ROWS = n_lat // GRID_W
    cx = ctx
    for l in range(DEPTH):
        last = l == DEPTH - 1
        ml = [m[:, None, :] for m in _modulation(c, w_mod[l], b_mod[l])]
        mc = _modulation(c_ctx, w_mod[l], b_mod[l])
        h_l = _rms(x, g_pre_mix[l]) * (1.0 + ml[1]) + ml[0]
        h_c = _rms(cx, g_pre_mix[l]) * (1.0 + mc[1]) + mc[0]
        o_c, o_l = _mixer(h_c, h_l, ROWS, w_in[l], rg_conv_w[l], rg_conv_b[l], rg_wa[l], rg_ba[l],
                          rg_wx[l], rg_bx[l], rg_lam[l], s5_lam_re[l], s5_lam_im[l], s5_log_step[l],
                          s5_b_re[l], s5_b_im[l], s5_c_re[l], s5_c_im[l], s5_d[l], s5_w_glu[l],
                          s5_b_glu[l], w_out[l], not last)
        x = x + ml[2] * _rms(o_l, g_post_mix[l])
        f_l = _rms(x, g_pre_ffn[l]) * (1.0 + ml[4]) + ml[3]
        x = x + ml[5] * _rms(_peer(f_l, peer_wq[l], peer_keys[l], peer_u[l], peer_v[l]), g_post_ffn[l])
        if not last:
            cx = cx + mc[2] * _rms(o_c, g_post_mix[l])
            f_c = _rms(cx, g_pre_ffn[l]) * (1.0 + mc[4]) + mc[3]
            cx = cx + mc[5] * _rms(_peer(f_c, peer_wq[l], peer_keys[l], peer_u[l], peer_v[l]), g_post_ffn[l])
    return x
```

```python
import functools
import math

import jax
import jax.numpy as jnp
from jax import lax
from jax.experimental import pallas as pl
from jax.experimental.pallas import tpu as pltpu

F32 = jnp.float32
BF16 = jnp.bfloat16

EPS = 1e-6
N_MOD = 6
GRID_W = 64
RG_HEADS = 8
RG_CONV = 4
RG_C = 8.0
S5_GROUP = 16
S5_STATE = 64
S5_CHUNK = 16
PEER_HEADS = 8
PEER_TOPK = 16
NEG_BIG = -1e30

SUBLANES = 8
LANES = 128
VMEM_LIMIT = 56 * 1024 * 1024


def _cparams(sem):
    return pltpu.CompilerParams(dimension_semantics=sem, vmem_limit_bytes=VMEM_LIMIT)


def _gelu(x):
    return 0.5 * x * (1.0 + jnp.tanh(0.7978845608028654 * (x + 0.044715 * (x * x * x))))


def _sigmoid(x):
    return 1.0 / (1.0 + jnp.exp(-x))


def _rms(x, g):
    return x * lax.rsqrt(jnp.mean(x * x, axis=-1, keepdims=True) + EPS) * g


def _dot(a, b):
    return jnp.dot(a, b, preferred_element_type=F32)


def _dot_nt(a, b):
    return lax.dot_general(a, b, (((1,), (1,)), ((), ())), preferred_element_type=F32)


def _mod_kernel(c_ref, w_ref, b_ref, o_ref):
    c = c_ref[...]
    s = c * _sigmoid(c)
    o_ref[...] = _dot(s.astype(BF16), w_ref[...].astype(BF16)) + b_ref[...]


def _modulation(cvecs, w_mod, b_mod):
    R, D = cvecs.shape
    N = w_mod.shape[1]
    tn = N // N_MOD
    return pl.pallas_call(
        _mod_kernel,
        grid=(N // tn,),
        in_specs=[pl.BlockSpec((R, D), lambda j: (0, 0)),
                  pl.BlockSpec((D, tn), lambda j: (0, j)),
                  pl.BlockSpec((1, tn), lambda j: (0, j))],
        out_specs=pl.BlockSpec((R, tn), lambda j: (0, j)),
        out_shape=jax.ShapeDtypeStruct((R, N), F32),
        compiler_params=_cparams(("arbitrary",)),
        name="modulation",
    )(cvecs, w_mod, b_mod.reshape(1, N))


def _inproj_kernel(x_ref, mod_ref, g_ref, w_ref, xr_ref, gg_ref, us_ref):
    x = x_ref[...]
    h = _rms(x, g_ref[...]) * (1.0 + mod_ref[1:2, :]) + mod_ref[0:1, :]
    p = _dot(h.astype(BF16), w_ref[...])
    w = xr_ref.shape[-1]
    xr_ref[...] = p[:, :w]
    gg_ref[...] = _gelu(p[:, w:2 * w])
    us_ref[...] = p[:, 2 * w:]


def _inproj(x, mods, mod_index, g, w_in_bf):
    B, L, D = x.shape
    W = w_in_bf.shape[1] // 3
    tt = min(512, L)
    out = jax.ShapeDtypeStruct((B, L, W), F32)
    ospec = pl.BlockSpec((None, tt, W), lambda b, i: (b, i, 0))
    return pl.pallas_call(
        _inproj_kernel,
        grid=(B, L // tt),
        in_specs=[pl.BlockSpec((None, tt, D), lambda b, i: (b, i, 0)),
                  pl.BlockSpec((None, 8, D), lambda b, i: (mod_index(b), 0, 0)),
                  pl.BlockSpec((1, D), lambda b, i: (0, 0)),
                  pl.BlockSpec((D, 3 * W), lambda b, i: (0, 0))],
        out_specs=[ospec, ospec, ospec],
        out_shape=[out, out, out],
        compiler_params=_cparams(("parallel", "parallel")),
        name="inproj",
    )(x, mods, g.reshape(1, D), w_in_bf)


def _rglru_kernel(rev, ntc, nt, cur_ref, prev_ref, next_ref, cw_ref, cb_ref, wg_ref, bg_ref,
                  lam_ref, h_ref, carry_sc, a_sc, b_sc, hh_sc, pp_sc):
    i = pl.program_id(1)
    tc, w = cur_ref.shape
    seg = tc // SUBLANES
    if rev:
        t = jnp.where(i < ntc, ntc - 1 - i, nt - 1 - (i - ntc))
    else:
        t = i

    @pl.when(i == 0)
    def _():
        carry_sc[...] = jnp.zeros_like(carry_sc)

    no_prev = jnp.logical_or(t == 0, t == ntc)
    no_next = jnp.logical_or(t == ntc - 1, t == nt - 1)
    cur = cur_ref[...]
    prev = jnp.where(no_prev, 0.0, prev_ref[...])
    nxt = jnp.where(no_next, 0.0, next_ref[...])
    ext = jnp.concatenate([prev, cur, nxt], axis=0)
    left = RG_CONV // 2
    xc = cb_ref[...]
    for k in range(RG_CONV):
        off = SUBLANES - left + k
        xc = xc + ext[off:off + tc, :] * cw_ref[k:k + 1, :]

    gates = _dot(xc.astype(BF16), wg_ref[...]) + bg_ref[...]
    r = _sigmoid(gates[:, :w])
    ig = _sigmoid(gates[:, w:])
    lam = lam_ref[...]
    softplus = jnp.maximum(-lam, 0.0) + jnp.log(1.0 + jnp.exp(-jnp.abs(lam)))
    log_a = -RG_C * r * softplus
    a = jnp.exp(log_a)
    b = jnp.sqrt(jnp.maximum(1.0 - a * a, 0.0)) * (ig * xc)
    for j in range(w // LANES):
        cols = slice(j * LANES, (j + 1) * LANES)
        a_sc[j] = a[:, cols]
        b_sc[j] = b[:, cols]

    for j in range(w // LANES):
        cols = slice(j * LANES, (j + 1) * LANES)
        hh = jnp.zeros((SUBLANES, LANES), F32)
        pp = jnp.ones((SUBLANES, LANES), F32)
        for k in (range(seg - 1, -1, -1) if rev else range(seg)):
            rows = pl.ds(k, SUBLANES, stride=seg)
            ak = a_sc[j, rows, :]
            hh = ak * hh + b_sc[j, rows, :]
            pp = pp * ak
            hh_sc[j, rows, :] = hh
            pp_sc[j, rows, :] = pp
        c = carry_sc[0:1, cols]
        cin = [None] * SUBLANES
        for s in (range(SUBLANES - 1, -1, -1) if rev else range(SUBLANES)):
            cin[s] = c
            c = hh[s:s + 1, :] + pp[s:s + 1, :] * c
        carry_sc[:, cols] = jnp.broadcast_to(c, (SUBLANES, LANES))
        for s in range(SUBLANES):
            rows = slice(s * seg, (s + 1) * seg)
            h_ref[rows, cols] = hh_sc[j, rows, :] + pp_sc[j, rows, :] * cin[s]


def _rglru_dir(xr_all, lc, rev, cw, cb, wg_bf, bg, lam):
    B, LT, W = xr_all.shape
    tc = 256
    nt = LT // tc
    ntc = lc // tc
    hb = tc // SUBLANES

    def tile(i):
        if rev:
            return jnp.where(i < ntc, ntc - 1 - i, nt - 1 - (i - ntc))
        return i

    nblk8 = LT // SUBLANES
    kern = functools.partial(_rglru_kernel, rev, ntc, nt)
    return pl.pallas_call(
        kern,
        grid=(B, nt),
        in_specs=[pl.BlockSpec((None, tc, W), lambda b, i: (b, tile(i), 0)),
                  pl.BlockSpec((None, SUBLANES, W),
                               lambda b, i: (b, jnp.maximum(tile(i) * hb - 1, 0), 0)),
                  pl.BlockSpec((None, SUBLANES, W),
                               lambda b, i: (b, jnp.minimum((tile(i) + 1) * hb, nblk8 - 1), 0)),
                  pl.BlockSpec((SUBLANES, W), lambda b, i: (0, 0)),
                  pl.BlockSpec((1, W), lambda b, i: (0, 0)),
                  pl.BlockSpec((W, 2 * W), lambda b, i: (0, 0)),
                  pl.BlockSpec((1, 2 * W), lambda b, i: (0, 0)),
                  pl.BlockSpec((1, W), lambda b, i: (0, 0))],
        out_specs=pl.BlockSpec((None, tc, W), lambda b, i: (b, tile(i), 0)),
        out_shape=jax.ShapeDtypeStruct((B, LT, W), F32),
        scratch_shapes=[pltpu.VMEM((SUBLANES, W), F32)] + [pltpu.VMEM((W // LANES, tc, LANES), F32)] * 4,
        compiler_params=_cparams(("parallel", "arbitrary")),
        name="rglru_bwd" if rev else "rglru_fwd",
    )(xr_all, xr_all, xr_all, cw, cb, wg_bf, bg, lam)


def _s5_kernel(nb, ncc, ncl, u_ref, kt_ref, min_ref, mout_ref, aq_ref, y_ref,
               xf_sc, xb_sc, sf_sc, sb_sc):
    nc = 2 * ncc + ncl
    half = S5_CHUNK * S5_GROUP
    ns = 2 * S5_STATE
    u = u_ref[...]
    xf_sc[...] = _dot(u, min_ref[0])
    xb_sc[...] = _dot(u, min_ref[1])

    def step(x_sc, s_sc, c, d, re, im):
        rows = slice(c * nb, (c + 1) * nb)
        s_sc[rows, :] = jnp.concatenate([re, im], axis=1)
        aqr = aq_ref[d, 0:1, :]
        aqi = aq_ref[d, 1:2, :]
        x = x_sc[rows, :]
        return (aqr * re - aqi * im + x[:, :ns], aqr * im + aqi * re + x[:, ns:])

    z = jnp.zeros((nb, ns), F32)
    fre, fim, bre, bim = z, z, z, z
    for k in range(ncc + ncl):
        fre, fim = step(xf_sc, sf_sc, k, 0, fre, fim)
        bre, bim = step(xb_sc, sb_sc, nc - 1 - k, 1, bre, bim)

    lat = slice(ncc * nb, (ncc + ncl) * nb)
    ul = u[lat, :]
    y = _dot(sf_sc[lat, :].astype(BF16), mout_ref[0]) + _dot(sb_sc[lat, :].astype(BF16), mout_ref[1])
    y_ref[...] = y + jnp.concatenate(
        [_dot(ul[:, :half], kt_ref[0]), _dot(ul[:, half:], kt_ref[1])], axis=1)


def _s5(u_gp, nb, ncc, ncl, kt, mn, mout, aq):
    GP, R, C = u_gp.shape
    half = C // 2
    ns2 = 4 * S5_STATE
    kern = functools.partial(_s5_kernel, nb, ncc, ncl)
    return pl.pallas_call(
        kern,
        grid=(GP,),
        in_specs=[pl.BlockSpec((None, R, C), lambda g: (g, 0, 0)),
                  pl.BlockSpec((None, 2, half, half), lambda g: (g, 0, 0, 0)),
                  pl.BlockSpec((None, 2, C, ns2), lambda g: (g, 0, 0, 0)),
                  pl.BlockSpec((None, 2, ns2, C), lambda g: (g, 0, 0, 0)),
                  pl.BlockSpec((None, 2, SUBLANES, ns2 // 2), lambda g: (g, 0, 0, 0))],
        out_specs=pl.BlockSpec((None, ncl * nb, C), lambda g: (g, 0, 0)),
        out_shape=jax.ShapeDtypeStruct((GP, ncl * nb, C), F32),
        scratch_shapes=[pltpu.VMEM((R, ns2), F32)] * 4,
        compiler_params=_cparams(("parallel",)),
        name="s5",
    )(u_gp, kt, mn, mout, aq)


def _s5_operators(lam_re, lam_im, log_step, b_re, b_im, c_re, c_im):
    hi = lax.Precision.HIGHEST
    Q = S5_CHUNK
    G, P = lam_re.shape[1], lam_re.shape[2]
    H = b_re.shape[-1]
    lre = jnp.minimum(lam_re, -1e-4)
    step = jnp.exp(log_step)[..., None]
    den = lre * lre + lam_im * lam_im
    mag = jnp.exp(lre * step)
    ar = mag * jnp.cos(lam_im * step)
    ai = mag * jnp.sin(lam_im * step)
    nr, ni = ar - 1.0, ai
    cr = (nr * lre + ni * lam_im) / den
    ci = (ni * lre - nr * lam_im) / den
    bbr = cr[..., None] * b_re - ci[..., None] * b_im
    bbi = cr[..., None] * b_im + ci[..., None] * b_re
    k = jnp.arange(Q + 1, dtype=F32)[:, None, None, None]
    pmag = jnp.exp(k * (lre * step)[None])
    pr = pmag * jnp.cos(k * (lam_im * step)[None])
    pi = pmag * jnp.sin(k * (lam_im * step)[None])
    car = c_re[None] * pr[:, :, :, None, :] - c_im[None] * pi[:, :, :, None, :]
    cai = c_re[None] * pi[:, :, :, None, :] + c_im[None] * pr[:, :, :, None, :]
    abr = pr[..., None] * bbr[None] - pi[..., None] * bbi[None]
    abi = pr[..., None] * bbi[None] + pi[..., None] * bbr[None]
    kl = (jnp.einsum('kdgop,dgpi->kdgoi', car[:Q], bbr, precision=hi)
          - jnp.einsum('kdgop,dgpi->kdgoi', cai[:Q], bbi, precision=hi))
    tin = jnp.arange(Q)[:, None]
    tout = jnp.arange(Q)[None, :]
    lag_f = tout - tin
    lag_b = tin - tout

    def toep(kd, lag):
        m = kd[jnp.clip(lag, 0, Q - 1)]
        m = jnp.where((lag >= 0)[:, :, None, None, None], m, 0.0)
        return m.transpose(2, 0, 4, 1, 3).reshape(G, Q * H, Q * H)

    kt = toep(kl[:, 0], lag_f) + toep(kl[:, 1], lag_b)
    kt = kt.reshape(G // 2, 2, Q * H, Q * H)

    def chunk_in(x):
        f = x[:Q, 0][::-1]
        b = x[:Q, 1]
        return jnp.stack([f, b]).transpose(0, 2, 1, 4, 3).reshape(2, G, Q * H, P)
    inr, ini = chunk_in(abr), chunk_in(abi)

    def chunk_out(x):
        f = x[1:Q + 1, 0]
        b = x[1:Q + 1, 1][::-1]
        return jnp.stack([f, b]).transpose(0, 2, 4, 1, 3).reshape(2, G, P, Q * H)
    outr, outi = chunk_out(car), -chunk_out(cai)

    GP = G // 2
    zin = jnp.zeros((2, GP, Q * H, P), F32)
    zout = jnp.zeros((2, GP, P, Q * H), F32)

    def pair_in(m):
        m = m.reshape(2, GP, 2, Q * H, P)
        top = jnp.concatenate([m[:, :, 0], zin], axis=-1)
        bot = jnp.concatenate([zin, m[:, :, 1]], axis=-1)
        return jnp.concatenate([top, bot], axis=-2)

    def pair_out(m):
        m = m.reshape(2, GP, 2, P, Q * H)
        top = jnp.concatenate([m[:, :, 0], zout], axis=-1)
        bot = jnp.concatenate([zout, m[:, :, 1]], axis=-1)
        return jnp.concatenate([top, bot], axis=-2)

    mn = jnp.concatenate([pair_in(inr), pair_in(ini)], axis=-1)
    mout = jnp.concatenate([pair_out(outr), pair_out(outi)], axis=-2)
    aqr = pr[Q].reshape(2, GP, 1, 2 * P)
    aqi = pi[Q].reshape(2, GP, 1, 2 * P)
    aq = jnp.concatenate([aqr, aqi, jnp.zeros((2, GP, SUBLANES - 2, 2 * P), F32)], axis=2)
    return (kt.astype(BF16), mn.transpose(1, 0, 2, 3).astype(BF16),
            mout.transpose(1, 0, 2, 3).astype(BF16), aq.transpose(1, 0, 2, 3))


def _outproj_kernel(x_ref, hf_ref, hb_ref, gg_ref, y_ref, us_ref, mod_ref, d_ref, wglu_ref,
                    bglu_ref, wout_ref, gpost_ref, gffn_ref, wq_ref, x1_ref, f_ref, q_ref):
    w = hf_ref.shape[-1]
    rg = (hf_ref[...] + hb_ref[...]) * gg_ref[...]
    z = _gelu(y_ref[...] + d_ref[...] * us_ref[...])
    glu = z * _sigmoid(_dot(z.astype(BF16), wglu_ref[...]) + bglu_ref[...])
    o = _dot(rg.astype(BF16), wout_ref[:w, :]) + _dot(glu.astype(BF16), wout_ref[w:, :])
    x1 = x_ref[...] + mod_ref[2:3, :] * _rms(o, gpost_ref[...])
    x1_ref[...] = x1
    f = (_rms(x1, gffn_ref[...]) * (1.0 + mod_ref[4:5, :]) + mod_ref[3:4, :]).astype(BF16)
    f_ref[...] = f
    q_ref[...] = _dot(f, wq_ref[...]).astype(BF16)


def _outproj(x, hf, hb, lc, gg, y, us, mods, d, wglu_bf, bglu, wout_bf, gpost, gffn, wq_bf):
    B, L, D = x.shape
    W = gg.shape[-1]
    NQ = wq_bf.shape[1]
    tt = 256
    off = lc // tt
    tok = lambda n: pl.BlockSpec((None, tt, n), lambda b, i: (b, i, 0))
    rgs = pl.BlockSpec((None, tt, W), lambda b, i: (b, i + off, 0))
    full = lambda r, c: pl.BlockSpec((r, c), lambda b, i: (0, 0))
    return pl.pallas_call(
        _outproj_kernel,
        grid=(B, L // tt),
        in_specs=[tok(D), rgs, rgs, tok(W), tok(W), tok(W),
                  pl.BlockSpec((None, 8, D), lambda b, i: (b, 0, 0)),
                  full(1, W), full(W, W), full(1, W), full(2 * W, D), full(1, D), full(1, D),
                  full(D, NQ)],
        out_specs=[tok(D), tok(D), tok(NQ)],
        out_shape=[jax.ShapeDtypeStruct((B, L, D), F32),
                   jax.ShapeDtypeStruct((B, L, D), BF16),
                   jax.ShapeDtypeStruct((B, L, NQ), BF16)],
        compiler_params=_cparams(("parallel", "parallel")),
        name="outproj",
    )(x, hf, hb, gg, y, us, mods, d.reshape(1, W), wglu_bf, bglu.reshape(1, W), wout_bf,
      gpost.reshape(1, D), gffn.reshape(1, D), wq_bf)


def _top_values(s, n):
    vals = []
    for r in range(n):
        m = jnp.max(s, axis=0, keepdims=True)
        vals.append(m)
        if r + 1 < n:
            s = jnp.where(s == m, NEG_BIG, s)
    return vals


def _peer_kernel(f_ref, q_ref, x1_ref, mod_ref, gpost_ref, keys_ref, u_ref, vt_ref, o_ref,
                 nal_sc, pak_sc, sb_sc, pb_sc, acc_sc):
    e = pl.program_id(2)
    ne = pl.num_programs(2)
    nk = sb_sc.shape[1]
    ec = u_ref.shape[0]
    n_top = PEER_TOPK + 1
    f = f_ref[...]

    @pl.when(e == 0)
    def _():
        acc_sc[...] = jnp.zeros_like(acc_sc)
        hd = q_ref.shape[1] // PEER_HEADS // 2
        for h in range(PEER_HEADS):
            qa = q_ref[:, (2 * h) * hd:(2 * h + 1) * hd]
            qb = q_ref[:, (2 * h + 1) * hd:(2 * h + 2) * hd]
            sa = _dot_nt(keys_ref[h, 0], qa)
            sb = _dot_nt(keys_ref[h, 1], qb)
            va = _top_values(sa, n_top)
            vb = _top_values(sb, n_top)
            cands = [va[i] + vb[j] for i in range(n_top) for j in range(n_top)
                     if (i + 1) * (j + 1) <= n_top]
            pad = (-len(cands)) % SUBLANES
            cands += [jnp.full_like(va[0], NEG_BIG)] * pad
            best = _top_values(jnp.concatenate(cands, axis=0), n_top)
            z = jnp.ones_like(best[0])
            for v in best[1:PEER_TOPK]:
                z = z + jnp.exp(v - best[0])
            tmid = 0.5 * (best[PEER_TOPK - 1] + best[PEER_TOPK])
            nal_sc[h] = tmid - sa
            pak_sc[h] = jnp.exp(sa - va[0]) / z
            sb_sc[h] = sb
            pb_sc[h] = jnp.exp(sb - vb[0])

    s = _dot_nt(u_ref[...], f)
    g = _gelu(s)
    blocks = []
    for j in range(ec // nk):
        ea = e * (ec // nk) + j
        wt = jnp.zeros((nk, f.shape[0]), F32)
        for h in range(PEER_HEADS):
            row_n = nal_sc[h, pl.ds(ea, 1), :]
            row_p = pak_sc[h, pl.ds(ea, 1), :]
            wt = wt + jnp.where(sb_sc[h] > row_n, row_p * pb_sc[h], 0.0)
        blocks.append(wt)
    act = (g * jnp.concatenate(blocks, axis=0)).astype(BF16)
    acc_sc[...] += _dot(vt_ref[...], act)

    @pl.when(e == ne - 1)
    def _():
        y = acc_sc[...].T
        o_ref[...] = x1_ref[...] + mod_ref[5:6, :] * _rms(y, gpost_ref[...])


def _peer(f, q, x1, mods, gpost, keys_bf, u_bf, vt_bf):
    B, L, D = f.shape
    NQ = q.shape[-1]
    E = u_bf.shape[0]
    nk = keys_bf.shape[2]
    tt = min(512, L)
    ec = min(1024, E)
    tok = lambda n: pl.BlockSpec((None, tt, n), lambda b, i, e: (b, i, 0))
    return pl.pallas_call(
        _peer_kernel,
        grid=(B, L // tt, E // ec),
        in_specs=[tok(D), tok(NQ), tok(D),
                  pl.BlockSpec((None, 8, D), lambda b, i, e: (b, 0, 0)),
                  pl.BlockSpec((1, D), lambda b, i, e: (0, 0)),
                  pl.BlockSpec(keys_bf.shape, lambda b, i, e: (0, 0, 0, 0)),
                  pl.BlockSpec((ec, D), lambda b, i, e: (e, 0)),
                  pl.BlockSpec((D, ec), lambda b, i, e: (0, e))],
        out_specs=tok(D),
        out_shape=jax.ShapeDtypeStruct((B, L, D), F32),
        scratch_shapes=[pltpu.VMEM((PEER_HEADS, nk, tt), F32)] * 4 + [pltpu.VMEM((D, tt), F32)],
        compiler_params=_cparams(("parallel", "parallel", "arbitrary")),
        name="peer",
    )(f, q, x1, mods, gpost.reshape(1, D), keys_bf, u_bf, vt_bf)


def _block_diag(w):
    H, n, _ = w.shape
    eye = jnp.eye(H, dtype=w.dtype)
    return (eye[:, None, :, None] * w[:, :, None, :]).reshape(H * n, H * n)


def _to_chunks(u):
    B, L, W = u.shape
    gp = W // (2 * S5_GROUP)
    u = u.reshape(B, L // S5_CHUNK, S5_CHUNK, gp, 2, S5_GROUP)
    u = u.transpose(3, 1, 0, 4, 2, 5)
    return u.reshape(gp, L // S5_CHUNK, B, 2 * S5_CHUNK * S5_GROUP)


def _from_chunks(y, B):
    gp, R, C = y.shape
    nc = R // B
    y = y.reshape(gp, nc, B, 2, S5_CHUNK, S5_GROUP)
    y = y.transpose(2, 1, 4, 0, 3, 5)
    return y.reshape(B, nc * S5_CHUNK, gp * 2 * S5_GROUP)


def kernel(x, c, ctx, c_ctx, w_mod, b_mod, g_pre_mix, g_post_mix, g_pre_ffn, g_post_ffn, w_in, rg_conv_w, rg_conv_b, rg_wa, rg_ba, rg_wx, rg_bx, rg_lam, s5_lam_re, s5_lam_im, s5_log_step, s5_b_re, s5_b_im, s5_c_re, s5_c_im, s5_d, s5_w_glu, s5_b_glu, w_out, peer_wq, peer_keys, peer_u, peer_v):
    B, L, D = x.shape
    Lc = ctx.shape[1]
    assert w_mod.shape[0] == 1, "single-layer block"
    assert L % 512 == 0 or L == 256
    assert Lc % 256 == 0 and L % (GRID_W * S5_CHUNK) == 0
    rows = L // GRID_W
    W = w_in.shape[2] // 3

    nb8 = -(-(B + 1) // 8) * 8
    cvecs = jnp.zeros((nb8, D), F32).at[:B].set(c).at[B].set(c_ctx)
    mods = _modulation(cvecs, w_mod[0], b_mod[0])
    mods = jnp.pad(mods.reshape(nb8, N_MOD, D), ((0, 0), (0, 8 - N_MOD), (0, 0)))

    w_in_bf = w_in[0].astype(BF16)
    xr_l, gg_l, us_l = _inproj(x, mods, lambda b: b, g_pre_mix[0], w_in_bf)
    xr_c, _, us_c = _inproj(ctx, mods, lambda b: B, g_pre_mix[0], w_in_bf)

    xr_all = jnp.concatenate([xr_c, xr_l], axis=1)
    cw = jnp.pad(rg_conv_w[0], ((0, SUBLANES - RG_CONV), (0, 0)))
    cb = rg_conv_b[0].reshape(1, W)
    hs = []
    for d in range(2):
        wg = jnp.concatenate([_block_diag(rg_wa[0, d]), _block_diag(rg_wx[0, d])], axis=1).astype(BF16)
        bg = jnp.concatenate([rg_ba[0, d], rg_bx[0, d]]).reshape(1, 2 * W)
        hs.append(_rglru_dir(xr_all, Lc, d == 1, cw, cb, wg, bg, rg_lam[0, d].reshape(1, W)))

    us_cm = us_l.reshape(B, rows, GRID_W, W).transpose(0, 2, 1, 3).reshape(B, L, W)
    uc = _to_chunks(us_c.astype(BF16))
    ul = _to_chunks(us_cm.astype(BF16))
    u_gp = jnp.concatenate([uc, ul, uc], axis=1)
    ncc, ncl = Lc // S5_CHUNK, L // S5_CHUNK
    u_gp = u_gp.reshape(u_gp.shape[0], (2 * ncc + ncl) * B, u_gp.shape[-1])
    kt, mn, mout, aq = _s5_operators(s5_lam_re[0], s5_lam_im[0], s5_log_step[0], s5_b_re[0],
                                     s5_b_im[0], s5_c_re[0], s5_c_im[0])
    y_gp = _s5(u_gp, B, ncc, ncl, kt, mn, mout, aq)
    y_cm = _from_chunks(y_gp, B)
    y_l = y_cm.reshape(B, GRID_W, rows, W).transpose(0, 2, 1, 3).reshape(B, L, W)

    x1, f, q = _outproj(x, hs[0], hs[1], Lc, gg_l, y_l, us_l, mods, s5_d[0],
                        s5_w_glu[0].astype(BF16), s5_b_glu[0], w_out[0].astype(BF16),
                        g_post_mix[0], g_pre_ffn[0], peer_wq[0].astype(BF16))

    return _peer(f, q, x1, mods, g_post_ffn[0], peer_keys[0].astype(BF16),
                 peer_u[0].astype(BF16), peer_v[0].T.astype(BF16))
```

```python
import functools

import jax
import jax.numpy as jnp
from jax import lax
from jax.experimental import pallas as pl
from jax.experimental.pallas import tpu as pltpu

F32 = jnp.float32
BF16 = jnp.bfloat16

EPS = 1e-6
N_MOD = 6
GRID_W = 64
RG_HEADS = 8
RG_CONV = 4
RG_C = 8.0
S5_GROUP = 16
S5_STATE = 64
S5_CHUNK = 16
PEER_HEADS = 8
PEER_TOPK = 16
NEG_BIG = -1e30

SUBLANES = 8
LANES = 128
VMEM_LIMIT = 56 * 1024 * 1024


def _cparams(sem):
    return pltpu.CompilerParams(dimension_semantics=sem, vmem_limit_bytes=VMEM_LIMIT)


def _gelu(x):
    return 0.5 * x * (1.0 + jnp.tanh(0.7978845608028654 * (x + 0.044715 * (x * x * x))))


def _sigmoid(x):
    return 1.0 / (1.0 + jnp.exp(-x))


def _rms(x, g):
    return x * lax.rsqrt(jnp.mean(x * x, axis=-1, keepdims=True) + EPS) * g


def _dot(a, b):
    return jnp.dot(a, b, preferred_element_type=F32)


def _dot_nt(a, b):
    return lax.dot_general(a, b, (((1,), (1,)), ((), ())), preferred_element_type=F32)


def _mod_kernel(c_ref, w_ref, b_ref, o_ref):
    c = c_ref[...]
    s = c * _sigmoid(c)
    o_ref[...] = _dot(s.astype(BF16), w_ref[...].astype(BF16)) + b_ref[...]


def _modulation(cvecs, w_mod, b_mod):
    R, D = cvecs.shape
    N = w_mod.shape[1]
    tn = N // N_MOD
    return pl.pallas_call(
        _mod_kernel,
        grid=(N // tn,),
        in_specs=[pl.BlockSpec((R, D), lambda j: (0, 0)),
                  pl.BlockSpec((D, tn), lambda j: (0, j)),
                  pl.BlockSpec((1, tn), lambda j: (0, j))],
        out_specs=pl.BlockSpec((R, tn), lambda j: (0, j)),
        out_shape=jax.ShapeDtypeStruct((R, N), F32),
        compiler_params=_cparams(("arbitrary",)),
        name="modulation",
    )(cvecs, w_mod, b_mod.reshape(1, N))


def _inproj_kernel(x_ref, mod_ref, g_ref, w_ref, xr_ref, gg_ref, us_ref):
    x = x_ref[...]
    h = _rms(x, g_ref[...]) * (1.0 + mod_ref[1:2, :]) + mod_ref[0:1, :]
    p = _dot(h.astype(BF16), w_ref[...])
    w = xr_ref.shape[-1]
    xr_ref[...] = p[:, :w]
    gg_ref[...] = _gelu(p[:, w:2 * w])
    us_ref[...] = p[:, 2 * w:]


def _inproj(x, mods, mod_index, g, w_in_bf):
    B, L, D = x.shape
    W = w_in_bf.shape[1] // 3
    tt = min(512, L)
    out = jax.ShapeDtypeStruct((B, L, W), F32)
    ospec = pl.BlockSpec((None, tt, W), lambda b, i: (b, i, 0))
    return pl.pallas_call(
        _inproj_kernel,
        grid=(B, L // tt),
        in_specs=[pl.BlockSpec((None, tt, D), lambda b, i: (b, i, 0)),
                  pl.BlockSpec((None, 8, D), lambda b, i: (mod_index(b), 0, 0)),
                  pl.BlockSpec((1, D), lambda b, i: (0, 0)),
                  pl.BlockSpec((D, 3 * W), lambda b, i: (0, 0))],
        out_specs=[ospec, ospec, ospec],
        out_shape=[out, out, out],
        compiler_params=_cparams(("parallel", "parallel")),
        name="inproj",
    )(x, mods, g.reshape(1, D), w_in_bf)


def _rglru_kernel(rev, ntc, nt, cur_ref, prev_ref, next_ref, cw_ref, cb_ref, wg_ref, bg_ref,
                  lam_ref, h_ref, carry_sc, a_sc, b_sc, hh_sc, pp_sc):
    i = pl.program_id(1)
    tc, w = cur_ref.shape
    seg = tc // SUBLANES
    if rev:
        t = jnp.where(i < ntc, ntc - 1 - i, nt - 1 - (i - ntc))
    else:
        t = i

    @pl.when(i == 0)
    def _():
        carry_sc[...] = jnp.zeros_like(carry_sc)

    no_prev = jnp.logical_or(t == 0, t == ntc)
    no_next = jnp.logical_or(t == ntc - 1, t == nt - 1)
    cur = cur_ref[...]
    prev = jnp.where(no_prev, 0.0, prev_ref[...])
    nxt = jnp.where(no_next, 0.0, next_ref[...])
    ext = jnp.concatenate([prev, cur, nxt], axis=0)
    left = RG_CONV // 2
    xc = cb_ref[...]
    for k in range(RG_CONV):
        off = SUBLANES - left + k
        xc = xc + ext[off:off + tc, :] * cw_ref[k:k + 1, :]

    gates = _dot(xc.astype(BF16), wg_ref[...]) + bg_ref[...]
    r = _sigmoid(gates[:, :w])
    ig = _sigmoid(gates[:, w:])
    lam = lam_ref[...]
    softplus = jnp.maximum(-lam, 0.0) + jnp.log(1.0 + jnp.exp(-jnp.abs(lam)))
    log_a = -RG_C * r * softplus
    a = jnp.exp(log_a)
    b = jnp.sqrt(jnp.maximum(1.0 - a * a, 0.0)) * (ig * xc)
    for j in range(w // LANES):
        cols = slice(j * LANES, (j + 1) * LANES)
        a_sc[j] = a[:, cols]
        b_sc[j] = b[:, cols]

    for j in range(w // LANES):
        cols = slice(j * LANES, (j + 1) * LANES)
        hh = jnp.zeros((SUBLANES, LANES), F32)
        pp = jnp.ones((SUBLANES, LANES), F32)
        for k in (range(seg - 1, -1, -1) if rev else range(seg)):
            rows = pl.ds(k, SUBLANES, stride=seg)
            ak = a_sc[j, rows, :]
            hh = ak * hh + b_sc[j, rows, :]
            pp = pp * ak
            hh_sc[j, rows, :] = hh
            pp_sc[j, rows, :] = pp
        c = carry_sc[0:1, cols]
        cin = [None] * SUBLANES
        for s in (range(SUBLANES - 1, -1, -1) if rev else range(SUBLANES)):
            cin[s] = c
            c = hh[s:s + 1, :] + pp[s:s + 1, :] * c
        carry_sc[:, cols] = jnp.broadcast_to(c, (SUBLANES, LANES))
        for s in range(SUBLANES):
            rows = slice(s * seg, (s + 1) * seg)
            h_ref[rows, cols] = hh_sc[j, rows, :] + pp_sc[j, rows, :] * cin[s]


def _rglru_dir(xr_all, lc, rev, cw, cb, wg_bf, bg, lam):
    B, LT, W = xr_all.shape
    tc = 256
    nt = LT // tc
    ntc = lc // tc
    hb = tc // SUBLANES

    def tile(i):
        if rev:
            return jnp.where(i < ntc, ntc - 1 - i, nt - 1 - (i - ntc))
        return i

    nblk8 = LT // SUBLANES
    kern = functools.partial(_rglru_kernel, rev, ntc, nt)
    return pl.pallas_call(
        kern,
        grid=(B, nt),
        in_specs=[pl.BlockSpec((None, tc, W), lambda b, i: (b, tile(i), 0)),
                  pl.BlockSpec((None, SUBLANES, W),
                               lambda b, i: (b, jnp.maximum(tile(i) * hb - 1, 0), 0)),
                  pl.BlockSpec((None, SUBLANES, W),
                               lambda b, i: (b, jnp.minimum((tile(i) + 1) * hb, nblk8 - 1), 0)),
                  pl.BlockSpec((SUBLANES, W), lambda b, i: (0, 0)),
                  pl.BlockSpec((1, W), lambda b, i: (0, 0)),
                  pl.BlockSpec((W, 2 * W), lambda b, i: (0, 0)),
                  pl.BlockSpec((1, 2 * W), lambda b, i: (0, 0)),
                  pl.BlockSpec((1, W), lambda b, i: (0, 0))],
        out_specs=pl.BlockSpec((None, tc, W), lambda b, i: (b, tile(i), 0)),
        out_shape=jax.ShapeDtypeStruct((B, LT, W), F32),
        scratch_shapes=[pltpu.VMEM((SUBLANES, W), F32)] + [pltpu.VMEM((W // LANES, tc, LANES), F32)] * 4,
        compiler_params=_cparams(("parallel", "arbitrary")),
        name="rglru_bwd" if rev else "rglru_fwd",
    )(xr_all, xr_all, xr_all, cw, cb, wg_bf, bg, lam)


def _s5_chunk_rows(xc_ref, xl_ref):
    ncc = xc_ref.shape[0] // S5_CHUNK
    cols = []
    for j in range(S5_CHUNK):
        parts = [xc_ref[pl.ds(j, ncc, stride=S5_CHUNK), :]]
        parts += [xl_ref[m, j * GRID_W:(j + 1) * GRID_W, :] for m in range(xl_ref.shape[0])]
        cols.append(jnp.concatenate(parts, axis=0))
    return jnp.concatenate(cols, axis=1).astype(BF16)


def _s5_in_kernel(xc_ref, xl_ref, min_ref, z_ref, xs_ref):
    z = _s5_chunk_rows(xc_ref, xl_ref)
    z_ref[...] = z
    xs_ref[...] = _dot(z, min_ref[...])


def _s5_scan_kernel(ncc, mc, xs_ref, aq_ref, s_ref):
    nb, nr, _ = xs_ref.shape
    ns = xs_ref.shape[2] // 4
    ncl = nr - ncc

    def lat_row(c):
        return ncc + (c % mc) * GRID_W + c // mc

    def advance(row, d, re, im):
        lo = 2 * d * ns
        for b in range(nb):
            s_ref[b, pl.ds(row, 1), lo:lo + ns] = re[b:b + 1, :]
            s_ref[b, pl.ds(row, 1), lo + ns:lo + 2 * ns] = im[b:b + 1, :]
        xr = jnp.concatenate([xs_ref[b, pl.ds(row, 1), lo:lo + ns] for b in range(nb)], axis=0)
        xi = jnp.concatenate([xs_ref[b, pl.ds(row, 1), lo + ns:lo + 2 * ns] for b in range(nb)],
                             axis=0)
        aqr = aq_ref[2 * d:2 * d + 1, :]
        aqi = aq_ref[2 * d + 1:2 * d + 2, :]
        return aqr * re - aqi * im + xr, aqr * im + aqi * re + xi

    def body(k, carry):
        fre, fim, bre, bim = carry
        in_ctx = k < ncc
        row_f = jnp.where(in_ctx, k, lat_row(jnp.maximum(k - ncc, 0)))
        row_b = jnp.where(in_ctx, ncc - 1 - k, lat_row(jnp.minimum(ncl - 1 - (k - ncc), ncl - 1)))
        fre, fim = advance(row_f, 0, fre, fim)
        bre, bim = advance(row_b, 1, bre, bim)
        return fre, fim, bre, bim

    z = jnp.zeros((nb, ns), F32)
    lax.fori_loop(0, nr, body, (z, z, z, z))


def _s5_out_kernel(ncc, z_ref, s_ref, km_ref, y_ref):
    lhs = jnp.concatenate([z_ref[ncc:, :], s_ref[ncc:, :].astype(BF16)], axis=1)
    yt = _dot(lhs, km_ref[...])
    for m in range(y_ref.shape[0]):
        for jj in range(y_ref.shape[1] // GRID_W):
            y_ref[m, jj * GRID_W:(jj + 1) * GRID_W, :] = (
                yt[m * GRID_W:(m + 1) * GRID_W, jj * LANES:(jj + 1) * LANES])


def _s5(us_c, us_l, min_q, km_q, aq_q):
    B, Lc, W = us_c.shape
    L = us_l.shape[1]
    nq = W // LANES
    ncc, ncl = Lc // S5_CHUNK, L // S5_CHUNK
    nr = ncc + ncl
    mc = L // (GRID_W * S5_CHUNK)
    kq = S5_CHUNK * LANES
    rows_m = S5_CHUNK * GRID_W
    us_l4 = us_l.reshape(B, mc, rows_m, W)
    z, xs = pl.pallas_call(
        _s5_in_kernel,
        grid=(nq, B),
        in_specs=[pl.BlockSpec((None, Lc, LANES), lambda q, b: (b, 0, q)),
                  pl.BlockSpec((None, mc, rows_m, LANES), lambda q, b: (b, 0, 0, q)),
                  pl.BlockSpec((None, kq, kq), lambda q, b: (q, 0, 0))],
        out_specs=[pl.BlockSpec((None, None, nr, kq), lambda q, b: (q, b, 0, 0))] * 2,
        out_shape=[jax.ShapeDtypeStruct((nq, B, nr, kq), BF16),
                   jax.ShapeDtypeStruct((nq, B, nr, kq), F32)],
        compiler_params=_cparams(("parallel", "parallel")),
        name="s5_in",
    )(us_c, us_l4, min_q)
    s = pl.pallas_call(
        functools.partial(_s5_scan_kernel, ncc, mc),
        grid=(nq,),
        in_specs=[pl.BlockSpec((None, B, nr, kq), lambda q: (q, 0, 0, 0)),
                  pl.BlockSpec((None, SUBLANES, kq // 4), lambda q: (q, 0, 0))],
        out_specs=pl.BlockSpec((None, B, nr, kq), lambda q: (q, 0, 0, 0)),
        out_shape=jax.ShapeDtypeStruct((nq, B, nr, kq), F32),
        compiler_params=_cparams(("parallel",)),
        name="s5_scan",
    )(xs, aq_q)
    nt = 2
    jt = S5_CHUNK // nt
    y4 = pl.pallas_call(
        functools.partial(_s5_out_kernel, ncc),
        grid=(nq, nt, B),
        in_specs=[pl.BlockSpec((None, None, nr, kq), lambda q, t, b: (q, b, 0, 0)),
                  pl.BlockSpec((None, None, nr, kq), lambda q, t, b: (q, b, 0, 0)),
                  pl.BlockSpec((None, 2 * kq, kq // nt), lambda q, t, b: (q, 0, t))],
        out_specs=pl.BlockSpec((None, mc, jt * GRID_W, LANES), lambda q, t, b: (b, 0, t, q)),
        out_shape=jax.ShapeDtypeStruct((B, mc, rows_m, W), F32),
        compiler_params=_cparams(("parallel", "parallel", "parallel")),
        name="s5_out",
    )(z, s, km_q)
    return y4.reshape(B, L, W)


def _s5_operators(lam_re, lam_im, log_step, b_re, b_im, c_re, c_im):
    hi = lax.Precision.HIGHEST
    Q = S5_CHUNK
    G, P = lam_re.shape[1], lam_re.shape[2]
    H = b_re.shape[-1]
    GL = LANES // H
    NQ = G // GL
    lre = jnp.minimum(lam_re, -1e-4)
    step = jnp.exp(log_step)[..., None]
    den = lre * lre + lam_im * lam_im
    mag = jnp.exp(lre * step)
    ar = mag * jnp.cos(lam_im * step)
    ai = mag * jnp.sin(lam_im * step)
    nr, ni = ar - 1.0, ai
    cr = (nr * lre + ni * lam_im) / den
    ci = (ni * lre - nr * lam_im) / den
    bbr = cr[..., None] * b_re - ci[..., None] * b_im
    bbi = cr[..., None] * b_im + ci[..., None] * b_re
    k = jnp.arange(Q + 1, dtype=F32)[:, None, None, None]
    pmag = jnp.exp(k * (lre * step)[None])
    pr = pmag * jnp.cos(k * (lam_im * step)[None])
    pi = pmag * jnp.sin(k * (lam_im * step)[None])
    car = c_re[None] * pr[:, :, :, None, :] - c_im[None] * pi[:, :, :, None, :]
    cai = c_re[None] * pi[:, :, :, None, :] + c_im[None] * pr[:, :, :, None, :]
    abr = pr[..., None] * bbr[None] - pi[..., None] * bbi[None]
    abi = pr[..., None] * bbi[None] + pi[..., None] * bbr[None]
    kl = (jnp.einsum('kdgop,dgpi->kdgoi', car[:Q], bbr, precision=hi)
          - jnp.einsum('kdgop,dgpi->kdgoi', cai[:Q], bbi, precision=hi))
    tin = jnp.arange(Q)[:, None]
    tout = jnp.arange(Q)[None, :]

    def toep(kd, lag):
        m = kd[jnp.clip(lag, 0, Q - 1)]
        m = jnp.where((lag >= 0)[:, :, None, None, None], m, 0.0)
        return m.transpose(2, 0, 4, 1, 3)

    kt = toep(kl[:, 0], tout - tin) + toep(kl[:, 1], tin - tout)
    eye = jnp.eye(GL, dtype=F32)
    kq = Q * LANES
    k_q = jnp.einsum('ab,qajhkm->qjahkbm', eye, kt.reshape(NQ, GL, Q, H, Q, H)).reshape(NQ, kq, kq)

    def chunk_in(x):
        f = x[:Q, 0][::-1]
        b = x[:Q, 1]
        return jnp.stack([f, b]).transpose(0, 2, 1, 4, 3).reshape(2, NQ, GL, Q, H, P)
    min_q = jnp.stack([jnp.einsum('ab,dqajhp->qjahdbp', eye, chunk_in(x)) for x in (abr, abi)],
                      axis=5).reshape(NQ, kq, kq)

    def chunk_out(x):
        f = x[1:Q + 1, 0]
        b = x[1:Q + 1, 1][::-1]
        return jnp.stack([f, b]).transpose(0, 2, 4, 1, 3).reshape(2, NQ, GL, P, Q, H)
    mout_q = jnp.stack([jnp.einsum('ab,dqapkm->qdapkbm', eye, x)
                        for x in (chunk_out(car), -chunk_out(cai))], axis=2).reshape(NQ, kq, kq)
    km_q = jnp.concatenate([k_q, mout_q], axis=1)
    aq = jnp.stack([pr[Q, 0], pi[Q, 0], pr[Q, 1], pi[Q, 1]], axis=0)
    aq = aq.reshape(4, NQ, GL * P).transpose(1, 0, 2)
    aq = jnp.pad(aq, ((0, 0), (0, SUBLANES - 4), (0, 0)))
    return min_q.astype(BF16), km_q.astype(BF16), aq


def _outproj_kernel(x_ref, hf_ref, hb_ref, gg_ref, y_ref, us_ref, mod_ref, d_ref, wglu_ref,
                    bglu_ref, wout_ref, gpost_ref, gffn_ref, wq_ref, x1_ref, f_ref, q_ref):
    w = hf_ref.shape[-1]
    rg = (hf_ref[...] + hb_ref[...]) * gg_ref[...]
    z = _gelu(y_ref[...] + d_ref[...] * us_ref[...])
    glu = z * _sigmoid(_dot(z.astype(BF16), wglu_ref[...]) + bglu_ref[...])
    o = _dot(rg.astype(BF16), wout_ref[:w, :]) + _dot(glu.astype(BF16), wout_ref[w:, :])
    x1 = x_ref[...] + mod_ref[2:3, :] * _rms(o, gpost_ref[...])
    x1_ref[...] = x1
    f = (_rms(x1, gffn_ref[...]) * (1.0 + mod_ref[4:5, :]) + mod_ref[3:4, :]).astype(BF16)
    f_ref[...] = f
    q_ref[...] = _dot(f, wq_ref[...]).astype(BF16)


def _outproj(x, hf, hb, lc, gg, y, us, mods, d, wglu_bf, bglu, wout_bf, gpost, gffn, wq_bf):
    B, L, D = x.shape
    W = gg.shape[-1]
    NQ = wq_bf.shape[1]
    tt = 256
    off = lc // tt
    tok = lambda n: pl.BlockSpec((None, tt, n), lambda b, i: (b, i, 0))
    rgs = pl.BlockSpec((None, tt, W), lambda b, i: (b, i + off, 0))
    full = lambda r, c: pl.BlockSpec((r, c), lambda b, i: (0, 0))
    return pl.pallas_call(
        _outproj_kernel,
        grid=(B, L // tt),
        in_specs=[tok(D), rgs, rgs, tok(W), tok(W), tok(W),
                  pl.BlockSpec((None, 8, D), lambda b, i: (b, 0, 0)),
                  full(1, W), full(W, W), full(1, W), full(2 * W, D), full(1, D), full(1, D),
                  full(D, NQ)],
        out_specs=[tok(D), tok(D), tok(NQ)],
        out_shape=[jax.ShapeDtypeStruct((B, L, D), F32),
                   jax.ShapeDtypeStruct((B, L, D), BF16),
                   jax.ShapeDtypeStruct((B, L, NQ), BF16)],
        compiler_params=_cparams(("parallel", "parallel")),
        name="outproj",
    )(x, hf, hb, gg, y, us, mods, d.reshape(1, W), wglu_bf, bglu.reshape(1, W), wout_bf,
      gpost.reshape(1, D), gffn.reshape(1, D), wq_bf)


def _top_values(s, n, with_rank=False):
    vals = []
    rank = jnp.full(s.shape, float(n), F32) if with_rank else None
    for r in range(n):
        m = jnp.max(s, axis=0, keepdims=True)
        vals.append(m)
        eq = s == m
        if with_rank:
            rank = jnp.where(eq, float(r), rank)
        if r + 1 < n:
            s = jnp.where(eq, NEG_BIG, s)
    return (vals, rank) if with_rank else vals


def _peer_kernel(f_ref, q_ref, x1_ref, mod_ref, gpost_ref, keys_ref, u_ref, vt_ref, o_ref,
                 nsel_sc, pak_sc, rb_sc, pb_sc, s0_sc, s1_sc, act0_sc, act1_sc, acc_sc):
    e = pl.program_id(2)
    ne = pl.num_programs(2)
    nk = rb_sc.shape[1]
    ec = s0_sc.shape[0]
    npc = ec // nk
    n_chunks = 2 * (ne - 1)
    n_top = PEER_TOPK + 1
    f = f_ref[...]

    @pl.when(e == 0)
    def _():
        acc_sc[...] = jnp.zeros_like(acc_sc)
        s1_sc[...] = jnp.zeros_like(s1_sc)
        act0_sc[...] = jnp.zeros_like(act0_sc)
        act1_sc[...] = jnp.zeros_like(act1_sc)
        hd = q_ref.shape[1] // PEER_HEADS // 2
        for h in range(PEER_HEADS):
            qa = q_ref[:, (2 * h) * hd:(2 * h + 1) * hd]
            qb = q_ref[:, (2 * h + 1) * hd:(2 * h + 2) * hd]
            sa = _dot_nt(keys_ref[h, 0], qa)
            sb = _dot_nt(keys_ref[h, 1], qb)
            va = _top_values(sa, n_top)
            vb, rank_b = _top_values(sb, n_top, with_rank=True)
            cands = [va[i] + vb[j] for i in range(n_top) for j in range(n_top)
                     if (i + 1) * (j + 1) <= n_top]
            pad = (-len(cands)) % SUBLANES
            cands += [jnp.full_like(va[0], NEG_BIG)] * pad
            best = _top_values(jnp.concatenate(cands, axis=0), n_top)
            z = jnp.ones_like(best[0])
            for v in best[1:PEER_TOPK]:
                z = z + jnp.exp(v - best[0])
            tmid = 0.5 * (best[PEER_TOPK - 1] + best[PEER_TOPK])
            nsel = jnp.zeros_like(sa)
            for j in range(n_top):
                nsel = nsel + jnp.where(sa > tmid - vb[j], 1.0, 0.0)
            nsel_sc[h] = nsel
            pak_sc[h] = jnp.exp(sa - va[0]) / z
            rb_sc[h] = rank_b.astype(BF16)
            pb_sc[h] = jnp.exp(sb - vb[0]).astype(BF16)

    tt = f.shape[0]

    def gate(s_sc, act_sc, chunk):
        for j in range(npc):
            ea = chunk * npc + j
            rows = slice(j * nk, (j + 1) * nk)
            g = _gelu(s_sc[rows, :]).astype(BF16)
            wt = jnp.zeros((nk, tt), BF16)
            for h in range(PEER_HEADS):
                ns = nsel_sc[h, pl.ds(ea, 1), :].astype(BF16)
                pa = pak_sc[h, pl.ds(ea, 1), :].astype(BF16)
                wt = wt + jnp.where(rb_sc[h] < ns, pa * pb_sc[h], 0.0)
            act_sc[rows, :] = g * wt

    acc_sc[...] += _dot(vt_ref[:, :ec], act0_sc[...])
    gate(s1_sc, act1_sc, jnp.maximum(2 * e - 1, 0))
    s0_sc[...] = _dot_nt(u_ref[:ec, :], f)
    acc_sc[...] += _dot(vt_ref[:, ec:], act1_sc[...])
    gate(s0_sc, act0_sc, jnp.minimum(2 * e, n_chunks - 1))
    s1_sc[...] = _dot_nt(u_ref[ec:, :], f)

    @pl.when(e == ne - 1)
    def _():
        y = acc_sc[...].T
        o_ref[...] = x1_ref[...] + mod_ref[5:6, :] * _rms(y, gpost_ref[...])


def _peer(f, q, x1, mods, gpost, keys_bf, u_bf, vt_bf):
    B, L, D = f.shape
    NQ = q.shape[-1]
    E = u_bf.shape[0]
    nk = keys_bf.shape[2]
    tt = min(512, L)
    ec = min(1024, E // 2)
    npair = E // (2 * ec)
    tok = lambda n: pl.BlockSpec((None, tt, n), lambda b, i, e: (b, i, 0))
    return pl.pallas_call(
        _peer_kernel,
        grid=(B, L // tt, npair + 1),
        in_specs=[tok(D), tok(NQ), tok(D),
                  pl.BlockSpec((None, 8, D), lambda b, i, e: (b, 0, 0)),
                  pl.BlockSpec((1, D), lambda b, i, e: (0, 0)),
                  pl.BlockSpec(keys_bf.shape, lambda b, i, e: (0, 0, 0, 0)),
                  pl.BlockSpec((2 * ec, D), lambda b, i, e: (jnp.minimum(e, npair - 1), 0)),
                  pl.BlockSpec((D, 2 * ec), lambda b, i, e: (0, jnp.maximum(e - 1, 0)))],
        out_specs=tok(D),
        out_shape=jax.ShapeDtypeStruct((B, L, D), F32),
        scratch_shapes=[pltpu.VMEM((PEER_HEADS, nk, tt), F32)] * 2
        + [pltpu.VMEM((PEER_HEADS, nk, tt), BF16)] * 2
        + [pltpu.VMEM((ec, tt), F32)] * 2 + [pltpu.VMEM((ec, tt), BF16)] * 2
        + [pltpu.VMEM((D, tt), F32)],
        compiler_params=_cparams(("parallel", "parallel", "arbitrary")),
        name="peer",
    )(f, q, x1, mods, gpost.reshape(1, D), keys_bf, u_bf, vt_bf)


def _block_diag(w):
    H, n, _ = w.shape
    eye = jnp.eye(H, dtype=w.dtype)
    return (eye[:, None, :, None] * w[:, :, None, :]).reshape(H * n, H * n)


def kernel(x, c, ctx, c_ctx, w_mod, b_mod, g_pre_mix, g_post_mix, g_pre_ffn, g_post_ffn, w_in, rg_conv_w, rg_conv_b, rg_wa, rg_ba, rg_wx, rg_bx, rg_lam, s5_lam_re, s5_lam_im, s5_log_step, s5_b_re, s5_b_im, s5_c_re, s5_c_im, s5_d, s5_w_glu, s5_b_glu, w_out, peer_wq, peer_keys, peer_u, peer_v):
    B, L, D = x.shape
    Lc = ctx.shape[1]
    assert w_mod.shape[0] == 1, "single-layer block"
    assert L % 512 == 0 or L == 256
    assert Lc % 256 == 0 and L % (GRID_W * S5_CHUNK) == 0
    W = w_in.shape[2] // 3

    nb8 = -(-(B + 1) // 8) * 8
    cvecs = jnp.zeros((nb8, D), F32).at[:B].set(c).at[B].set(c_ctx)
    mods = _modulation(cvecs, w_mod[0], b_mod[0])
    mods = jnp.pad(mods.reshape(nb8, N_MOD, D), ((0, 0), (0, 8 - N_MOD), (0, 0)))

    w_in_bf = w_in[0].astype(BF16)
    xr_l, gg_l, us_l = _inproj(x, mods, lambda b: b, g_pre_mix[0], w_in_bf)
    xr_c, _, us_c = _inproj(ctx, mods, lambda b: B, g_pre_mix[0], w_in_bf)

    xr_all = jnp.concatenate([xr_c, xr_l], axis=1)
    cw = jnp.pad(rg_conv_w[0], ((0, SUBLANES - RG_CONV), (0, 0)))
    cb = rg_conv_b[0].reshape(1, W)
    hs = []
    for d in range(2):
        wg = jnp.concatenate([_block_diag(rg_wa[0, d]), _block_diag(rg_wx[0, d])], axis=1).astype(BF16)
        bg = jnp.concatenate([rg_ba[0, d], rg_bx[0, d]]).reshape(1, 2 * W)
        hs.append(_rglru_dir(xr_all, Lc, d == 1, cw, cb, wg, bg, rg_lam[0, d].reshape(1, W)))

    min_q, km_q, aq_q = _s5_operators(s5_lam_re[0], s5_lam_im[0], s5_log_step[0], s5_b_re[0],
                                      s5_b_im[0], s5_c_re[0], s5_c_im[0])
    y_l = _s5(us_c, us_l, min_q, km_q, aq_q)

    x1, f, q = _outproj(x, hs[0], hs[1], Lc, gg_l, y_l, us_l, mods, s5_d[0],
                        s5_w_glu[0].astype(BF16), s5_b_glu[0], w_out[0].astype(BF16),
                        g_post_mix[0], g_pre_ffn[0], peer_wq[0].astype(BF16))

    return _peer(f, q, x1, mods, g_post_ffn[0], peer_keys[0].astype(BF16),
                 peer_u[0].astype(BF16), peer_v[0].T.astype(BF16))
```

```python
import functools

import jax
import jax.numpy as jnp
from jax import lax
from jax.experimental import pallas as pl
from jax.experimental.pallas import tpu as pltpu

F32 = jnp.float32
BF16 = jnp.bfloat16

EPS = 1e-6
N_MOD = 6
GRID_W = 64
RG_HEADS = 8
RG_CONV = 4
RG_C = 8.0
S5_GROUP = 16
S5_STATE = 64
S5_CHUNK = 16
PEER_HEADS = 8
PEER_TOPK = 16
NEG_BIG = -1e30

SUBLANES = 8
LANES = 128
VMEM_LIMIT = 56 * 1024 * 1024


def _cparams(sem):
    return pltpu.CompilerParams(dimension_semantics=sem, vmem_limit_bytes=VMEM_LIMIT)


def _gelu(x):
    return 0.5 * x * (1.0 + jnp.tanh(0.7978845608028654 * (x + 0.044715 * (x * x * x))))


def _sigmoid(x):
    return 1.0 / (1.0 + jnp.exp(-x))


def _rms(x, g):
    return x * lax.rsqrt(jnp.mean(x * x, axis=-1, keepdims=True) + EPS) * g


def _dot(a, b):
    return jnp.dot(a, b, preferred_element_type=F32)


def _dot_nt(a, b):
    return lax.dot_general(a, b, (((1,), (1,)), ((), ())), preferred_element_type=F32)


def _mod_kernel(c_ref, w_ref, b_ref, o_ref):
    c = c_ref[...]
    s = c * _sigmoid(c)
    o_ref[...] = _dot(s.astype(BF16), w_ref[...].astype(BF16)) + b_ref[...]


def _modulation(cvecs, w_mod, b_mod):
    R, D = cvecs.shape
    N = w_mod.shape[1]
    tn = N // N_MOD
    return pl.pallas_call(
        _mod_kernel,
        grid=(N // tn,),
        in_specs=[pl.BlockSpec((R, D), lambda j: (0, 0)),
                  pl.BlockSpec((D, tn), lambda j: (0, j)),
                  pl.BlockSpec((1, tn), lambda j: (0, j))],
        out_specs=pl.BlockSpec((R, tn), lambda j: (0, j)),
        out_shape=jax.ShapeDtypeStruct((R, N), F32),
        compiler_params=_cparams(("arbitrary",)),
        name="modulation",
    )(cvecs, w_mod, b_mod.reshape(1, N))


def _inproj_kernel(x_ref, mod_ref, g_ref, w_ref, xr_ref, gg_ref, us_ref):
    x = x_ref[...]
    h = _rms(x, g_ref[...]) * (1.0 + mod_ref[1:2, :]) + mod_ref[0:1, :]
    p = _dot(h.astype(BF16), w_ref[...])
    w = xr_ref.shape[-1]
    xr_ref[...] = p[:, :w]
    gg_ref[...] = _gelu(p[:, w:2 * w])
    us_ref[...] = p[:, 2 * w:]


def _inproj(x, mods, mod_index, g, w_in_bf):
    B, L, D = x.shape
    W = w_in_bf.shape[1] // 3
    tt = min(512, L)
    out = jax.ShapeDtypeStruct((B, L, W), F32)
    ospec = pl.BlockSpec((None, tt, W), lambda b, i: (b, i, 0))
    return pl.pallas_call(
        _inproj_kernel,
        grid=(B, L // tt),
        in_specs=[pl.BlockSpec((None, tt, D), lambda b, i: (b, i, 0)),
                  pl.BlockSpec((None, 8, D), lambda b, i: (mod_index(b), 0, 0)),
                  pl.BlockSpec((1, D), lambda b, i: (0, 0)),
                  pl.BlockSpec((D, 3 * W), lambda b, i: (0, 0))],
        out_specs=[ospec, ospec, ospec],
        out_shape=[out, out, out],
        compiler_params=_cparams(("parallel", "parallel")),
        name="inproj",
    )(x, mods, g.reshape(1, D), w_in_bf)


def _rglru_kernel(rev, ntc, nt, cur_ref, prev_ref, next_ref, cw_ref, cb_ref, wg_ref, bg_ref,
                  lam_ref, h_ref, carry_sc, a_sc, b_sc, hh_sc, pp_sc):
    i = pl.program_id(1)
    tc, w = cur_ref.shape
    seg = tc // SUBLANES
    if rev:
        t = jnp.where(i < ntc, ntc - 1 - i, nt - 1 - (i - ntc))
    else:
        t = i

    @pl.when(i == 0)
    def _():
        carry_sc[...] = jnp.zeros_like(carry_sc)

    no_prev = jnp.logical_or(t == 0, t == ntc)
    no_next = jnp.logical_or(t == ntc - 1, t == nt - 1)
    cur = cur_ref[...]
    prev = jnp.where(no_prev, 0.0, prev_ref[...])
    nxt = jnp.where(no_next, 0.0, next_ref[...])
    ext = jnp.concatenate([prev, cur, nxt], axis=0)
    left = RG_CONV // 2
    xc = cb_ref[...]
    for k in range(RG_CONV):
        off = SUBLANES - left + k
        xc = xc + ext[off:off + tc, :] * cw_ref[k:k + 1, :]

    gates = _dot(xc.astype(BF16), wg_ref[...]) + bg_ref[...]
    r = _sigmoid(gates[:, :w])
    ig = _sigmoid(gates[:, w:])
    lam = lam_ref[...]
    softplus = jnp.maximum(-lam, 0.0) + jnp.log(1.0 + jnp.exp(-jnp.abs(lam)))
    log_a = -RG_C * r * softplus
    a = jnp.exp(log_a)
    b = jnp.sqrt(jnp.maximum(1.0 - a * a, 0.0)) * (ig * xc)
    pitch = a_sc.shape[1] // SUBLANES
    for j in range(w // LANES):
        cols = slice(j * LANES, (j + 1) * LANES)
        for s in range(SUBLANES):
            a_sc[j, s * pitch:s * pitch + seg, :] = a[s * seg:(s + 1) * seg, cols]
            b_sc[j, s * pitch:s * pitch + seg, :] = b[s * seg:(s + 1) * seg, cols]

    for j in range(w // LANES):
        cols = slice(j * LANES, (j + 1) * LANES)
        hh = jnp.zeros((SUBLANES, LANES), F32)
        pp = jnp.ones((SUBLANES, LANES), F32)
        for k in (range(seg - 1, -1, -1) if rev else range(seg)):
            rows = pl.ds(k, SUBLANES, stride=pitch)
            ak = a_sc[j, rows, :]
            hh = ak * hh + b_sc[j, rows, :]
            pp = pp * ak
            hh_sc[j, rows, :] = hh
            pp_sc[j, rows, :] = pp
        c = carry_sc[0:1, cols]
        cin = [None] * SUBLANES
        for s in (range(SUBLANES - 1, -1, -1) if rev else range(SUBLANES)):
            cin[s] = c
            c = hh[s:s + 1, :] + pp[s:s + 1, :] * c
        carry_sc[:, cols] = jnp.broadcast_to(c, (SUBLANES, LANES))
        for s in range(SUBLANES):
            src = slice(s * pitch, s * pitch + seg)
            h_ref[s * seg:(s + 1) * seg, cols] = hh_sc[j, src, :] + pp_sc[j, src, :] * cin[s]


def _rglru_dir(xr_all, lc, rev, cw, cb, wg_bf, bg, lam):
    B, LT, W = xr_all.shape
    tc = 256
    nt = LT // tc
    ntc = lc // tc
    hb = tc // SUBLANES

    def tile(i):
        if rev:
            return jnp.where(i < ntc, ntc - 1 - i, nt - 1 - (i - ntc))
        return i

    nblk8 = LT // SUBLANES
    kern = functools.partial(_rglru_kernel, rev, ntc, nt)
    return pl.pallas_call(
        kern,
        grid=(B, nt),
        in_specs=[pl.BlockSpec((None, tc, W), lambda b, i: (b, tile(i), 0)),
                  pl.BlockSpec((None, SUBLANES, W),
                               lambda b, i: (b, jnp.maximum(tile(i) * hb - 1, 0), 0)),
                  pl.BlockSpec((None, SUBLANES, W),
                               lambda b, i: (b, jnp.minimum((tile(i) + 1) * hb, nblk8 - 1), 0)),
                  pl.BlockSpec((SUBLANES, W), lambda b, i: (0, 0)),
                  pl.BlockSpec((1, W), lambda b, i: (0, 0)),
                  pl.BlockSpec((W, 2 * W), lambda b, i: (0, 0)),
                  pl.BlockSpec((1, 2 * W), lambda b, i: (0, 0)),
                  pl.BlockSpec((1, W), lambda b, i: (0, 0))],
        out_specs=pl.BlockSpec((None, tc, W), lambda b, i: (b, tile(i), 0)),
        out_shape=jax.ShapeDtypeStruct((B, LT, W), F32),
        scratch_shapes=[pltpu.VMEM((SUBLANES, W), F32)]
        + [pltpu.VMEM((W // LANES, SUBLANES * (tc // SUBLANES + SUBLANES), LANES), F32)] * 4,
        compiler_params=_cparams(("parallel", "arbitrary")),
        name="rglru_bwd" if rev else "rglru_fwd",
    )(xr_all, xr_all, xr_all, cw, cb, wg_bf, bg, lam)


def _s5_chunk_rows(xc_ref, xl_ref):
    ncc = xc_ref.shape[0] // S5_CHUNK
    cols = []
    for j in range(S5_CHUNK):
        parts = [xc_ref[pl.ds(j, ncc, stride=S5_CHUNK), :]]
        parts += [xl_ref[m, j * GRID_W:(j + 1) * GRID_W, :] for m in range(xl_ref.shape[0])]
        cols.append(jnp.concatenate(parts, axis=0))
    return jnp.concatenate(cols, axis=1).astype(BF16)


def _s5_in_kernel(xc_ref, xl_ref, min_ref, z_ref, xs_ref):
    z = _s5_chunk_rows(xc_ref, xl_ref)
    z_ref[...] = z
    xs_ref[...] = _dot(z, min_ref[...])


def _s5_scan_kernel(ncc, mc, xs_ref, aq_ref, s_ref):
    nb, nr, _ = xs_ref.shape
    ns = xs_ref.shape[2] // 4
    ncl = nr - ncc

    def lat_row(c):
        return ncc + (c % mc) * GRID_W + c // mc

    def advance(row, d, re, im):
        lo = 2 * d * ns
        for b in range(nb):
            s_ref[b, pl.ds(row, 1), lo:lo + ns] = re[b:b + 1, :]
            s_ref[b, pl.ds(row, 1), lo + ns:lo + 2 * ns] = im[b:b + 1, :]
        xr = jnp.concatenate([xs_ref[b, pl.ds(row, 1), lo:lo + ns] for b in range(nb)], axis=0)
        xi = jnp.concatenate([xs_ref[b, pl.ds(row, 1), lo + ns:lo + 2 * ns] for b in range(nb)],
                             axis=0)
        aqr = aq_ref[2 * d:2 * d + 1, :]
        aqi = aq_ref[2 * d + 1:2 * d + 2, :]
        return aqr * re - aqi * im + xr, aqr * im + aqi * re + xi

    def body(k, carry):
        fre, fim, bre, bim = carry
        in_ctx = k < ncc
        row_f = jnp.where(in_ctx, k, lat_row(jnp.maximum(k - ncc, 0)))
        row_b = jnp.where(in_ctx, ncc - 1 - k, lat_row(jnp.minimum(ncl - 1 - (k - ncc), ncl - 1)))
        fre, fim = advance(row_f, 0, fre, fim)
        bre, bim = advance(row_b, 1, bre, bim)
        return fre, fim, bre, bim

    z = jnp.zeros((nb, ns), F32)
    lax.fori_loop(0, nr, body, (z, z, z, z))


def _s5_out_kernel(ncc, z_ref, s_ref, km_ref, y_ref):
    yt = _dot(z_ref[ncc:, :], km_ref[0]) + _dot(s_ref[ncc:, :].astype(BF16), km_ref[1])
    for m in range(y_ref.shape[0]):
        for jj in range(y_ref.shape[1] // GRID_W):
            y_ref[m, jj * GRID_W:(jj + 1) * GRID_W, :] = (
                yt[m * GRID_W:(m + 1) * GRID_W, jj * LANES:(jj + 1) * LANES])


S5_OP_K, S5_OP_OUT, S5_OP_IN = 0, 1, 2


def _s5_expand_kernel(gl, hsh, psh, c_ref, e_ref, o_ref):
    which = pl.program_id(1)
    x = _dot(c_ref[...], e_ref[...])
    tr = x.shape[0]
    row = lax.broadcasted_iota(jnp.int32, x.shape, 0) + pl.program_id(2) * tr
    col = lax.broadcasted_iota(jnp.int32, x.shape, 1)
    rsh = jnp.where(which == S5_OP_OUT, psh, hsh)
    csh = jnp.where(which == S5_OP_IN, psh, hsh)
    keep = ((row >> rsh) & (gl - 1)) == ((col >> csh) & (gl - 1))
    o_ref[...] = jnp.where(keep, x, 0.0).astype(BF16)


def _s5_expand(comp, spread, gl, hsh, psh):
    nq, nop, kq, kc = comp.shape
    tr = 512
    return pl.pallas_call(
        functools.partial(_s5_expand_kernel, gl, hsh, psh),
        grid=(nq, nop, kq // tr),
        in_specs=[pl.BlockSpec((None, None, tr, kc), lambda q, w, r: (q, w, r, 0)),
                  pl.BlockSpec((None, kc, kq), lambda q, w, r: (w // S5_OP_IN, 0, 0))],
        out_specs=pl.BlockSpec((None, None, tr, kq), lambda q, w, r: (q, w, r, 0)),
        out_shape=jax.ShapeDtypeStruct((nq, nop, kq, kq), BF16),
        compiler_params=_cparams(("parallel", "parallel", "parallel")),
        name="s5_expand",
    )(comp, spread)


def _s5(us_c, us_l, ops_q, aq_q):
    B, Lc, W = us_c.shape
    L = us_l.shape[1]
    nq = W // LANES
    ncc, ncl = Lc // S5_CHUNK, L // S5_CHUNK
    nr = ncc + ncl
    mc = L // (GRID_W * S5_CHUNK)
    kq = S5_CHUNK * LANES
    rows_m = S5_CHUNK * GRID_W
    us_l4 = us_l.reshape(B, mc, rows_m, W)
    z, xs = pl.pallas_call(
        _s5_in_kernel,
        grid=(nq, B),
        in_specs=[pl.BlockSpec((None, Lc, LANES), lambda q, b: (b, 0, q)),
                  pl.BlockSpec((None, mc, rows_m, LANES), lambda q, b: (b, 0, 0, q)),
                  pl.BlockSpec((None, None, kq, kq), lambda q, b: (q, S5_OP_IN, 0, 0))],
        out_specs=[pl.BlockSpec((None, None, nr, kq), lambda q, b: (q, b, 0, 0))] * 2,
        out_shape=[jax.ShapeDtypeStruct((nq, B, nr, kq), BF16),
                   jax.ShapeDtypeStruct((nq, B, nr, kq), F32)],
        compiler_params=_cparams(("parallel", "parallel")),
        name="s5_in",
    )(us_c, us_l4, ops_q)
    s = pl.pallas_call(
        functools.partial(_s5_scan_kernel, ncc, mc),
        grid=(nq,),
        in_specs=[pl.BlockSpec((None, B, nr, kq), lambda q: (q, 0, 0, 0)),
                  pl.BlockSpec((None, SUBLANES, kq // 4), lambda q: (q, 0, 0))],
        out_specs=pl.BlockSpec((None, B, nr, kq), lambda q: (q, 0, 0, 0)),
        out_shape=jax.ShapeDtypeStruct((nq, B, nr, kq), F32),
        compiler_params=_cparams(("parallel",)),
        name="s5_scan",
    )(xs, aq_q)
    nt = 2
    jt = S5_CHUNK // nt
    y4 = pl.pallas_call(
        functools.partial(_s5_out_kernel, ncc),
        grid=(nq, nt, B),
        in_specs=[pl.BlockSpec((None, None, nr, kq), lambda q, t, b: (q, b, 0, 0)),
                  pl.BlockSpec((None, None, nr, kq), lambda q, t, b: (q, b, 0, 0)),
                  pl.BlockSpec((None, 2, kq, kq // nt), lambda q, t, b: (q, 0, 0, t))],
        out_specs=pl.BlockSpec((None, mc, jt * GRID_W, LANES), lambda q, t, b: (b, 0, t, q)),
        out_shape=jax.ShapeDtypeStruct((B, mc, rows_m, W), F32),
        compiler_params=_cparams(("parallel", "parallel", "parallel")),
        name="s5_out",
    )(z, s, ops_q)
    return y4.reshape(B, L, W)


def _s5_operators(lam_re, lam_im, log_step, b_re, b_im, c_re, c_im):
    hi = lax.Precision.HIGHEST
    Q = S5_CHUNK
    G, P = lam_re.shape[1], lam_re.shape[2]
    H = b_re.shape[-1]
    GL = LANES // H
    NQ = G // GL
    lre = jnp.minimum(lam_re, -1e-4)
    step = jnp.exp(log_step)[..., None]
    den = lre * lre + lam_im * lam_im
    mag = jnp.exp(lre * step)
    ar = mag * jnp.cos(lam_im * step)
    ai = mag * jnp.sin(lam_im * step)
    nr, ni = ar - 1.0, ai
    cr = (nr * lre + ni * lam_im) / den
    ci = (ni * lre - nr * lam_im) / den
    bbr = cr[..., None] * b_re - ci[..., None] * b_im
    bbi = cr[..., None] * b_im + ci[..., None] * b_re
    k = jnp.arange(Q + 1, dtype=F32)[:, None, None, None]
    pmag = jnp.exp(k * (lre * step)[None])
    pr = pmag * jnp.cos(k * (lam_im * step)[None])
    pi = pmag * jnp.sin(k * (lam_im * step)[None])
    car = c_re[None] * pr[:, :, :, None, :] - c_im[None] * pi[:, :, :, None, :]
    cai = c_re[None] * pi[:, :, :, None, :] + c_im[None] * pr[:, :, :, None, :]
    abr = pr[..., None] * bbr[None] - pi[..., None] * bbi[None]
    abi = pr[..., None] * bbi[None] + pi[..., None] * bbr[None]
    kl = (jnp.einsum('kdgop,dgpi->kdgoi', car[:Q], bbr, precision=hi)
          - jnp.einsum('kdgop,dgpi->kdgoi', cai[:Q], bbi, precision=hi))
    tin = jnp.arange(Q)[:, None]
    tout = jnp.arange(Q)[None, :]

    def toep(kd, lag):
        m = kd[jnp.clip(lag, 0, Q - 1)]
        m = jnp.where((lag >= 0)[:, :, None, None, None], m, 0.0)
        return m.transpose(2, 0, 4, 1, 3)

    kt = toep(kl[:, 0], tout - tin) + toep(kl[:, 1], tin - tout)
    kq = Q * LANES
    kc = Q * H
    assert kc == 4 * P, "compact operators share one width"
    k_c = kt.reshape(NQ, GL, Q, H, kc).transpose(0, 2, 1, 3, 4).reshape(NQ, kq, kc)

    def chunk_in(x):
        f = x[:Q, 0][::-1]
        b = x[:Q, 1]
        return jnp.stack([f, b]).transpose(0, 2, 1, 4, 3).reshape(2, NQ, GL, Q, H, P)
    in_c = jnp.stack([chunk_in(abr), chunk_in(abi)], axis=1)
    in_c = in_c.transpose(2, 4, 3, 5, 0, 1, 6).reshape(NQ, kq, kc)

    def chunk_out(x):
        f = x[1:Q + 1, 0]
        b = x[1:Q + 1, 1][::-1]
        return jnp.stack([f, b]).transpose(0, 2, 4, 1, 3).reshape(2, NQ, GL, P, kc)
    out_c = jnp.stack([chunk_out(car), -chunk_out(cai)], axis=1)
    out_c = out_c.transpose(2, 0, 1, 3, 4, 5).reshape(NQ, kq, kc)

    comp = jnp.stack([k_c, out_c, in_c], axis=1).astype(BF16)
    eye = lambda n: jnp.eye(n, dtype=BF16)
    ones = jnp.ones((1, GL), BF16)
    spread = jnp.stack([jnp.kron(eye(Q), jnp.kron(ones, eye(H))),
                        jnp.kron(eye(4), jnp.kron(ones, eye(P)))])
    ops_q = _s5_expand(comp, spread, GL, H.bit_length() - 1, P.bit_length() - 1)
    aq = jnp.stack([pr[Q, 0], pi[Q, 0], pr[Q, 1], pi[Q, 1]], axis=0)
    aq = aq.reshape(4, NQ, GL * P).transpose(1, 0, 2)
    aq = jnp.pad(aq, ((0, 0), (0, SUBLANES - 4), (0, 0)))
    return ops_q, aq


def _outproj_kernel(x_ref, hf_ref, hb_ref, gg_ref, y_ref, us_ref, mod_ref, d_ref, wglu_ref,
                    bglu_ref, wout_ref, gpost_ref, gffn_ref, wq_ref, x1_ref, f_ref, q_ref):
    w = hf_ref.shape[-1]
    rg = (hf_ref[...] + hb_ref[...]) * gg_ref[...]
    z = _gelu(y_ref[...] + d_ref[...] * us_ref[...])
    glu = z * _sigmoid(_dot(z.astype(BF16), wglu_ref[...]) + bglu_ref[...])
    o = _dot(rg.astype(BF16), wout_ref[:w, :]) + _dot(glu.astype(BF16), wout_ref[w:, :])
    x1 = x_ref[...] + mod_ref[2:3, :] * _rms(o, gpost_ref[...])
    x1_ref[...] = x1
    f = (_rms(x1, gffn_ref[...]) * (1.0 + mod_ref[4:5, :]) + mod_ref[3:4, :]).astype(BF16)
    f_ref[...] = f
    q_ref[...] = _dot(f, wq_ref[...]).astype(BF16)


def _outproj(x, hf, hb, lc, gg, y, us, mods, d, wglu_bf, bglu, wout_bf, gpost, gffn, wq_bf):
    B, L, D = x.shape
    W = gg.shape[-1]
    NQ = wq_bf.shape[1]
    tt = 256
    off = lc // tt
    tok = lambda n: pl.BlockSpec((None, tt, n), lambda b, i: (b, i, 0))
    rgs = pl.BlockSpec((None, tt, W), lambda b, i: (b, i + off, 0))
    full = lambda r, c: pl.BlockSpec((r, c), lambda b, i: (0, 0))
    return pl.pallas_call(
        _outproj_kernel,
        grid=(B, L // tt),
        in_specs=[tok(D), rgs, rgs, tok(W), tok(W), tok(W),
                  pl.BlockSpec((None, 8, D), lambda b, i: (b, 0, 0)),
                  full(1, W), full(W, W), full(1, W), full(2 * W, D), full(1, D), full(1, D),
                  full(D, NQ)],
        out_specs=[tok(D), tok(D), tok(NQ)],
        out_shape=[jax.ShapeDtypeStruct((B, L, D), F32),
                   jax.ShapeDtypeStruct((B, L, D), BF16),
                   jax.ShapeDtypeStruct((B, L, NQ), BF16)],
        compiler_params=_cparams(("parallel", "parallel")),
        name="outproj",
    )(x, hf, hb, gg, y, us, mods, d.reshape(1, W), wglu_bf, bglu.reshape(1, W), wout_bf,
      gpost.reshape(1, D), gffn.reshape(1, D), wq_bf)


def _top_values(s, n, with_rank=False):
    vals = []
    rank = jnp.full(s.shape, float(n), F32) if with_rank else None
    for r in range(n):
        m = jnp.max(s, axis=0, keepdims=True)
        vals.append(m)
        eq = s == m
        if with_rank:
            rank = jnp.where(eq, float(r), rank)
        if r + 1 < n:
            s = jnp.where(eq, NEG_BIG, s)
    return (vals, rank) if with_rank else vals


def _peer_kernel(f_ref, q_ref, x1_ref, mod_ref, gpost_ref, keys_ref, u_ref, vt_ref, o_ref,
                 nsel_sc, pak_sc, rb_sc, pb_sc, s0_sc, s1_sc, act0_sc, act1_sc, acc_sc):
    e = pl.program_id(2)
    ne = pl.num_programs(2)
    nk = rb_sc.shape[1]
    ec = s0_sc.shape[0]
    npc = ec // nk
    n_chunks = 2 * (ne - 1)
    n_top = PEER_TOPK + 1
    f = f_ref[...]

    @pl.when(e == 0)
    def _():
        acc_sc[...] = jnp.zeros_like(acc_sc)
        s1_sc[...] = jnp.zeros_like(s1_sc)
        act0_sc[...] = jnp.zeros_like(act0_sc)
        act1_sc[...] = jnp.zeros_like(act1_sc)
        hd = q_ref.shape[1] // PEER_HEADS // 2
        for h in range(PEER_HEADS):
            qa = q_ref[:, (2 * h) * hd:(2 * h + 1) * hd]
            qb = q_ref[:, (2 * h + 1) * hd:(2 * h + 2) * hd]
            sa_all = _dot_nt(keys_ref[h, 0], qa)
            sb_all = _dot_nt(keys_ref[h, 1], qb)
            for lt in range(sa_all.shape[1] // LANES):
                cols = slice(lt * LANES, (lt + 1) * LANES)
                sa, sb = sa_all[:, cols], sb_all[:, cols]
                va = _top_values(sa, n_top)
                vb, rank_b = _top_values(sb, n_top, with_rank=True)
                cands = [va[i] + vb[j] for i in range(n_top) for j in range(n_top)
                         if (i + 1) * (j + 1) <= n_top]
                pad = (-len(cands)) % SUBLANES
                cands += [jnp.full_like(va[0], NEG_BIG)] * pad
                best = _top_values(jnp.concatenate(cands, axis=0), n_top)
                z = jnp.ones_like(best[0])
                for v in best[1:PEER_TOPK]:
                    z = z + jnp.exp(v - best[0])
                tmid = 0.5 * (best[PEER_TOPK - 1] + best[PEER_TOPK])
                nsel = jnp.zeros_like(sa)
                for j in range(n_top):
                    nsel = nsel + jnp.where(sa > tmid - vb[j], 1.0, 0.0)
                nsel_sc[h, :, cols] = nsel
                pak_sc[h, :, cols] = jnp.exp(sa - va[0]) * (0.5 / z)
                rb_sc[h, :, cols] = rank_b.astype(BF16)
                pb_sc[h, :, cols] = jnp.exp(sb - vb[0]).astype(BF16)

    tt = f.shape[0]
    tw = min(tt, 2 * LANES)

    def gate(s_sc, act_sc, chunk):
        for j in range(npc):
            ea = chunk * npc + j
            rows = slice(j * nk, (j + 1) * nk)
            for t0 in range(0, tt, tw):
                cols = slice(t0, t0 + tw)
                s = s_sc[rows, cols]
                t = jnp.tanh(s * (0.7978845608028654 + 0.035677408136300125 * (s * s)))
                g = (s + s * t).astype(BF16)
                wt = jnp.zeros((nk, tw), BF16)
                for h in range(PEER_HEADS):
                    ns = nsel_sc[h, pl.ds(ea, 1), cols].astype(BF16)
                    pa = pak_sc[h, pl.ds(ea, 1), cols].astype(BF16)
                    wt = wt + jnp.where(rb_sc[h, :, cols] < ns, pa * pb_sc[h, :, cols], 0.0)
                act_sc[rows, cols] = g * wt

    acc_sc[...] += _dot(vt_ref[:, :ec], act0_sc[...])
    gate(s1_sc, act1_sc, jnp.maximum(2 * e - 1, 0))
    s0_sc[...] = _dot_nt(u_ref[:ec, :], f)
    acc_sc[...] += _dot(vt_ref[:, ec:], act1_sc[...])
    gate(s0_sc, act0_sc, jnp.minimum(2 * e, n_chunks - 1))
    s1_sc[...] = _dot_nt(u_ref[ec:, :], f)

    @pl.when(e == ne - 1)
    def _():
        y = acc_sc[...].T
        o_ref[...] = x1_ref[...] + mod_ref[5:6, :] * _rms(y, gpost_ref[...])


def _peer(f, q, x1, mods, gpost, keys_bf, u_bf, vt_bf):
    B, L, D = f.shape
    NQ = q.shape[-1]
    E = u_bf.shape[0]
    nk = keys_bf.shape[2]
    tt = min(512, L)
    ec = min(1024, E // 2)
    npair = E // (2 * ec)
    tok = lambda n: pl.BlockSpec((None, tt, n), lambda b, i, e: (b, i, 0))
    return pl.pallas_call(
        _peer_kernel,
        grid=(B, L // tt, npair + 1),
        in_specs=[tok(D), tok(NQ), tok(D),
                  pl.BlockSpec((None, 8, D), lambda b, i, e: (b, 0, 0)),
                  pl.BlockSpec((1, D), lambda b, i, e: (0, 0)),
                  pl.BlockSpec(keys_bf.shape, lambda b, i, e: (0, 0, 0, 0)),
                  pl.BlockSpec((2 * ec, D), lambda b, i, e: (jnp.minimum(e, npair - 1), 0)),
                  pl.BlockSpec((D, 2 * ec), lambda b, i, e: (0, jnp.maximum(e - 1, 0)))],
        out_specs=tok(D),
        out_shape=jax.ShapeDtypeStruct((B, L, D), F32),
        scratch_shapes=[pltpu.VMEM((PEER_HEADS, nk, tt), F32)] * 2
        + [pltpu.VMEM((PEER_HEADS, nk, tt), BF16)] * 2
        + [pltpu.VMEM((ec, tt), F32)] * 2 + [pltpu.VMEM((ec, tt), BF16)] * 2
        + [pltpu.VMEM((D, tt), F32)],
        compiler_params=_cparams(("parallel", "parallel", "arbitrary")),
        name="peer",
    )(f, q, x1, mods, gpost.reshape(1, D), keys_bf, u_bf, vt_bf)


def _block_diag(w):
    H, n, _ = w.shape
    eye = jnp.eye(H, dtype=w.dtype)
    return (eye[:, None, :, None] * w[:, :, None, :]).reshape(H * n, H * n)


def kernel(x, c, ctx, c_ctx, w_mod, b_mod, g_pre_mix, g_post_mix, g_pre_ffn, g_post_ffn, w_in, rg_conv_w, rg_conv_b, rg_wa, rg_ba, rg_wx, rg_bx, rg_lam, s5_lam_re, s5_lam_im, s5_log_step, s5_b_re, s5_b_im, s5_c_re, s5_c_im, s5_d, s5_w_glu, s5_b_glu, w_out, peer_wq, peer_keys, peer_u, peer_v):
    B, L, D = x.shape
    Lc = ctx.shape[1]
    assert w_mod.shape[0] == 1, "single-layer block"
    assert L % 512 == 0 or L == 256
    assert Lc % 256 == 0 and L % (GRID_W * S5_CHUNK) == 0
    W = w_in.shape[2] // 3

    nb8 = -(-(B + 1) // 8) * 8
    cvecs = jnp.zeros((nb8, D), F32).at[:B].set(c).at[B].set(c_ctx)
    mods = _modulation(cvecs, w_mod[0], b_mod[0])
    mods = jnp.pad(mods.reshape(nb8, N_MOD, D), ((0, 0), (0, 8 - N_MOD), (0, 0)))

    w_in_bf = w_in[0].astype(BF16)
    xr_l, gg_l, us_l = _inproj(x, mods, lambda b: b, g_pre_mix[0], w_in_bf)
    xr_c, _, us_c = _inproj(ctx, mods, lambda b: B, g_pre_mix[0], w_in_bf)

    xr_all = jnp.concatenate([xr_c, xr_l], axis=1)
    cw = jnp.pad(rg_conv_w[0], ((0, SUBLANES - RG_CONV), (0, 0)))
    cb = rg_conv_b[0].reshape(1, W)
    hs = []
    for d in range(2):
        wg = jnp.concatenate([_block_diag(rg_wa[0, d]), _block_diag(rg_wx[0, d])], axis=1).astype(BF16)
        bg = jnp.concatenate([rg_ba[0, d], rg_bx[0, d]]).reshape(1, 2 * W)
        hs.append(_rglru_dir(xr_all, Lc, d == 1, cw, cb, wg, bg, rg_lam[0, d].reshape(1, W)))

    ops_q, aq_q = _s5_operators(s5_lam_re[0], s5_lam_im[0], s5_log_step[0], s5_b_re[0],
                                s5_b_im[0], s5_c_re[0], s5_c_im[0])
    y_l = _s5(us_c, us_l, ops_q, aq_q)

    x1, f, q = _outproj(x, hs[0], hs[1], Lc, gg_l, y_l, us_l, mods, s5_d[0],
                        s5_w_glu[0].astype(BF16), s5_b_glu[0], w_out[0].astype(BF16),
                        g_post_mix[0], g_pre_ffn[0], peer_wq[0].astype(BF16))

    return _peer(f, q, x1, mods, g_post_ffn[0], peer_keys[0].astype(BF16),
                 peer_u[0].astype(BF16), peer_v[0].T.astype(BF16))
```

```python
import functools

import jax
import jax.numpy as jnp
import numpy as np
from jax import lax
from jax.experimental import pallas as pl
from jax.experimental.pallas import tpu as pltpu

F32 = jnp.float32
BF16 = jnp.bfloat16

EPS = 1e-6
N_MOD = 6
GRID_W = 64
RG_HEADS = 8
RG_CONV = 4
RG_C = 8.0
S5_GROUP = 16
S5_STATE = 64
S5_CHUNK = 16
PEER_HEADS = 8
PEER_TOPK = 16
NEG_BIG = -(2.0 ** 100)

SUBLANES = 8
LANES = 128
VMEM_LIMIT = 56 * 1024 * 1024


def _cparams(sem):
    return pltpu.CompilerParams(dimension_semantics=sem, vmem_limit_bytes=VMEM_LIMIT)


def _gelu(x):
    return 0.5 * x * (1.0 + jnp.tanh(0.7978845608028654 * (x + 0.044715 * (x * x * x))))


def _sigmoid(x):
    return 1.0 / (1.0 + jnp.exp(-x))


def _rms(x, g):
    return x * lax.rsqrt(jnp.mean(x * x, axis=-1, keepdims=True) + EPS) * g


def _dot(a, b):
    return jnp.dot(a, b, preferred_element_type=F32)


def _dot_nt(a, b):
    return lax.dot_general(a, b, (((1,), (1,)), ((), ())), preferred_element_type=F32)


def _mod_kernel(c_ref, w_ref, b_ref, o_ref):
    c = c_ref[...]
    s = c * _sigmoid(c)
    o_ref[...] = _dot(s.astype(BF16), w_ref[...].astype(BF16)) + b_ref[...]


def _modulation(cvecs, w_mod, b_mod):
    R, D = cvecs.shape
    N = w_mod.shape[1]
    tn = N // N_MOD
    return pl.pallas_call(
        _mod_kernel,
        grid=(N // tn,),
        in_specs=[pl.BlockSpec((R, D), lambda j: (0, 0)),
                  pl.BlockSpec((D, tn), lambda j: (0, j)),
                  pl.BlockSpec((1, tn), lambda j: (0, j))],
        out_specs=pl.BlockSpec((R, tn), lambda j: (0, j)),
        out_shape=jax.ShapeDtypeStruct((R, N), F32),
        compiler_params=_cparams(("arbitrary",)),
        name="modulation",
    )(cvecs, w_mod, b_mod.reshape(1, N))


def _inproj_kernel(x_ref, mod_ref, g_ref, w_ref, xr_ref, gg_ref, us_ref):
    x = x_ref[...]
    h = _rms(x, g_ref[...]) * (1.0 + mod_ref[1:2, :]) + mod_ref[0:1, :]
    p = _dot(h.astype(BF16), w_ref[...])
    w = xr_ref.shape[-1]
    xr_ref[...] = p[:, :w]
    gg_ref[...] = _gelu(p[:, w:2 * w])
    us_ref[...] = p[:, 2 * w:]


def _inproj(x, mods, mod_index, g, w_in_bf):
    B, L, D = x.shape
    W = w_in_bf.shape[1] // 3
    tt = min(512, L)
    out = jax.ShapeDtypeStruct((B, L, W), F32)
    ospec = pl.BlockSpec((None, tt, W), lambda b, i: (b, i, 0))
    return pl.pallas_call(
        _inproj_kernel,
        grid=(B, L // tt),
        in_specs=[pl.BlockSpec((None, tt, D), lambda b, i: (b, i, 0)),
                  pl.BlockSpec((None, 8, D), lambda b, i: (mod_index(b), 0, 0)),
                  pl.BlockSpec((1, D), lambda b, i: (0, 0)),
                  pl.BlockSpec((D, 3 * W), lambda b, i: (0, 0))],
        out_specs=[ospec, ospec, ospec],
        out_shape=[out, out, out],
        compiler_params=_cparams(("parallel", "parallel")),
        name="inproj",
    )(x, mods, g.reshape(1, D), w_in_bf)


def _rglru_kernel(rev, ntc, nt, cur_ref, prev_ref, next_ref, cw_ref, cb_ref, wg_ref, bg_ref,
                  lam_ref, h_ref, carry_sc, a_sc, b_sc, hh_sc, pp_sc):
    i = pl.program_id(1)
    tc, w = cur_ref.shape
    seg = tc // SUBLANES
    if rev:
        t = jnp.where(i < ntc, ntc - 1 - i, nt - 1 - (i - ntc))
    else:
        t = i

    @pl.when(i == 0)
    def _():
        carry_sc[...] = jnp.zeros_like(carry_sc)

    no_prev = jnp.logical_or(t == 0, t == ntc)
    no_next = jnp.logical_or(t == ntc - 1, t == nt - 1)
    cur = cur_ref[...]
    prev = jnp.where(no_prev, 0.0, prev_ref[...])
    nxt = jnp.where(no_next, 0.0, next_ref[...])
    ext = jnp.concatenate([prev, cur, nxt], axis=0)
    left = RG_CONV // 2
    xc = cb_ref[...]
    for k in range(RG_CONV):
        off = SUBLANES - left + k
        xc = xc + ext[off:off + tc, :] * cw_ref[k:k + 1, :]

    gates = _dot(xc.astype(BF16), wg_ref[...]) + bg_ref[...]
    r = _sigmoid(gates[:, :w])
    ig = _sigmoid(gates[:, w:])
    lam = lam_ref[...]
    softplus = jnp.maximum(-lam, 0.0) + jnp.log(1.0 + jnp.exp(-jnp.abs(lam)))
    log_a = -RG_C * r * softplus
    a = jnp.exp(log_a)
    b = jnp.sqrt(jnp.maximum(1.0 - a * a, 0.0)) * (ig * xc)
    pitch = a_sc.shape[1] // SUBLANES
    for j in range(w // LANES):
        cols = slice(j * LANES, (j + 1) * LANES)
        for s in range(SUBLANES):
            a_sc[j, s * pitch:s * pitch + seg, :] = a[s * seg:(s + 1) * seg, cols]
            b_sc[j, s * pitch:s * pitch + seg, :] = b[s * seg:(s + 1) * seg, cols]

    for j in range(w // LANES):
        cols = slice(j * LANES, (j + 1) * LANES)
        hh = jnp.zeros((SUBLANES, LANES), F32)
        pp = jnp.ones((SUBLANES, LANES), F32)
        for k in (range(seg - 1, -1, -1) if rev else range(seg)):
            rows = pl.ds(k, SUBLANES, stride=pitch)
            ak = a_sc[j, rows, :]
            hh = ak * hh + b_sc[j, rows, :]
            pp = pp * ak
            hh_sc[j, rows, :] = hh
            pp_sc[j, rows, :] = pp
        c = carry_sc[0:1, cols]
        cin = [None] * SUBLANES
        for s in (range(SUBLANES - 1, -1, -1) if rev else range(SUBLANES)):
            cin[s] = c
            c = hh[s:s + 1, :] + pp[s:s + 1, :] * c
        carry_sc[:, cols] = jnp.broadcast_to(c, (SUBLANES, LANES))
        for s in range(SUBLANES):
            src = slice(s * pitch, s * pitch + seg)
            h_ref[s * seg:(s + 1) * seg, cols] = hh_sc[j, src, :] + pp_sc[j, src, :] * cin[s]


def _rglru_dir(xr_all, lc, rev, cw, cb, wg_bf, bg, lam):
    B, LT, W = xr_all.shape
    tc = 256
    nt = LT // tc
    ntc = lc // tc
    hb = tc // SUBLANES

    def tile(i):
        if rev:
            return jnp.where(i < ntc, ntc - 1 - i, nt - 1 - (i - ntc))
        return i

    nblk8 = LT // SUBLANES
    kern = functools.partial(_rglru_kernel, rev, ntc, nt)
    return pl.pallas_call(
        kern,
        grid=(B, nt),
        in_specs=[pl.BlockSpec((None, tc, W), lambda b, i: (b, tile(i), 0)),
                  pl.BlockSpec((None, SUBLANES, W),
                               lambda b, i: (b, jnp.maximum(tile(i) * hb - 1, 0), 0)),
                  pl.BlockSpec((None, SUBLANES, W),
                               lambda b, i: (b, jnp.minimum((tile(i) + 1) * hb, nblk8 - 1), 0)),
                  pl.BlockSpec((SUBLANES, W), lambda b, i: (0, 0)),
                  pl.BlockSpec((1, W), lambda b, i: (0, 0)),
                  pl.BlockSpec((W, 2 * W), lambda b, i: (0, 0)),
                  pl.BlockSpec((1, 2 * W), lambda b, i: (0, 0)),
                  pl.BlockSpec((1, W), lambda b, i: (0, 0))],
        out_specs=pl.BlockSpec((None, tc, W), lambda b, i: (b, tile(i), 0)),
        out_shape=jax.ShapeDtypeStruct((B, LT, W), F32),
        scratch_shapes=[pltpu.VMEM((SUBLANES, W), F32)]
        + [pltpu.VMEM((W // LANES, SUBLANES * (tc // SUBLANES + SUBLANES), LANES), F32)] * 4,
        compiler_params=_cparams(("parallel", "arbitrary")),
        name="rglru_bwd" if rev else "rglru_fwd",
    )(xr_all, xr_all, xr_all, cw, cb, wg_bf, bg, lam)


def _s5_chunk_rows(xc_ref, xl_ref):
    ncc = xc_ref.shape[0] // S5_CHUNK
    cols = []
    for j in range(S5_CHUNK):
        parts = [xc_ref[pl.ds(j, ncc, stride=S5_CHUNK), :]]
        parts += [xl_ref[m, j * GRID_W:(j + 1) * GRID_W, :] for m in range(xl_ref.shape[0])]
        cols.append(jnp.concatenate(parts, axis=0))
    return jnp.concatenate(cols, axis=1).astype(BF16)


def _s5_in_kernel(xc_ref, xl_ref, min_ref, z_ref, xs_ref):
    z = _s5_chunk_rows(xc_ref, xl_ref)
    z_ref[...] = z
    xs_ref[...] = _dot(z, min_ref[...])


def _s5_scan_kernel(ncc, mc, xs_ref, aq_ref, s_ref):
    nb, nr, _ = xs_ref.shape
    ns = xs_ref.shape[2] // 4
    ncl = nr - ncc

    def lat_row(c):
        return ncc + (c % mc) * GRID_W + c // mc

    def advance(row, d, re, im):
        lo = 2 * d * ns
        for b in range(nb):
            s_ref[b, pl.ds(row, 1), lo:lo + ns] = re[b:b + 1, :]
            s_ref[b, pl.ds(row, 1), lo + ns:lo + 2 * ns] = im[b:b + 1, :]
        xr = jnp.concatenate([xs_ref[b, pl.ds(row, 1), lo:lo + ns] for b in range(nb)], axis=0)
        xi = jnp.concatenate([xs_ref[b, pl.ds(row, 1), lo + ns:lo + 2 * ns] for b in range(nb)],
                             axis=0)
        aqr = aq_ref[2 * d:2 * d + 1, :]
        aqi = aq_ref[2 * d + 1:2 * d + 2, :]
        return aqr * re - aqi * im + xr, aqr * im + aqi * re + xi

    def body(k, carry):
        fre, fim, bre, bim = carry
        in_ctx = k < ncc
        row_f = jnp.where(in_ctx, k, lat_row(jnp.maximum(k - ncc, 0)))
        row_b = jnp.where(in_ctx, ncc - 1 - k, lat_row(jnp.minimum(ncl - 1 - (k - ncc), ncl - 1)))
        fre, fim = advance(row_f, 0, fre, fim)
        bre, bim = advance(row_b, 1, bre, bim)
        return fre, fim, bre, bim

    z = jnp.zeros((nb, ns), F32)
    lax.fori_loop(0, nr, body, (z, z, z, z))


def _s5_out_kernel(ncc, z_ref, s_ref, km_ref, y_ref):
    yt = _dot(z_ref[ncc:, :], km_ref[0]) + _dot(s_ref[ncc:, :].astype(BF16), km_ref[1])
    for m in range(y_ref.shape[0]):
        for jj in range(y_ref.shape[1] // GRID_W):
            y_ref[m, jj * GRID_W:(jj + 1) * GRID_W, :] = (
                yt[m * GRID_W:(m + 1) * GRID_W, jj * LANES:(jj + 1) * LANES])


S5_OP_K, S5_OP_OUT, S5_OP_IN = 0, 1, 2


def _s5_expand_kernel(gl, hsh, psh, c_ref, e_ref, o_ref):
    which = pl.program_id(1)
    x = _dot(c_ref[...], e_ref[...])
    tr = x.shape[0]
    row = lax.broadcasted_iota(jnp.int32, x.shape, 0) + pl.program_id(2) * tr
    col = lax.broadcasted_iota(jnp.int32, x.shape, 1)
    rsh = jnp.where(which == S5_OP_OUT, psh, hsh)
    csh = jnp.where(which == S5_OP_IN, psh, hsh)
    keep = ((row >> rsh) & (gl - 1)) == ((col >> csh) & (gl - 1))
    o_ref[...] = jnp.where(keep, x, 0.0).astype(BF16)


def _s5_expand(comp, spread, gl, hsh, psh):
    nq, nop, kq, kc = comp.shape
    tr = 512
    return pl.pallas_call(
        functools.partial(_s5_expand_kernel, gl, hsh, psh),
        grid=(nq, nop, kq // tr),
        in_specs=[pl.BlockSpec((None, None, tr, kc), lambda q, w, r: (q, w, r, 0)),
                  pl.BlockSpec((None, kc, kq), lambda q, w, r: (w // S5_OP_IN, 0, 0))],
        out_specs=pl.BlockSpec((None, None, tr, kq), lambda q, w, r: (q, w, r, 0)),
        out_shape=jax.ShapeDtypeStruct((nq, nop, kq, kq), BF16),
        compiler_params=_cparams(("parallel", "parallel", "parallel")),
        name="s5_expand",
    )(comp, spread)


def _s5(us_c, us_l, ops_q, aq_q):
    B, Lc, W = us_c.shape
    L = us_l.shape[1]
    nq = W // LANES
    ncc, ncl = Lc // S5_CHUNK, L // S5_CHUNK
    nr = ncc + ncl
    mc = L // (GRID_W * S5_CHUNK)
    kq = S5_CHUNK * LANES
    rows_m = S5_CHUNK * GRID_W
    us_l4 = us_l.reshape(B, mc, rows_m, W)
    z, xs = pl.pallas_call(
        _s5_in_kernel,
        grid=(nq, B),
        in_specs=[pl.BlockSpec((None, Lc, LANES), lambda q, b: (b, 0, q)),
                  pl.BlockSpec((None, mc, rows_m, LANES), lambda q, b: (b, 0, 0, q)),
                  pl.BlockSpec((None, None, kq, kq), lambda q, b: (q, S5_OP_IN, 0, 0))],
        out_specs=[pl.BlockSpec((None, None, nr, kq), lambda q, b: (q, b, 0, 0))] * 2,
        out_shape=[jax.ShapeDtypeStruct((nq, B, nr, kq), BF16),
                   jax.ShapeDtypeStruct((nq, B, nr, kq), F32)],
        compiler_params=_cparams(("parallel", "parallel")),
        name="s5_in",
    )(us_c, us_l4, ops_q)
    s = pl.pallas_call(
        functools.partial(_s5_scan_kernel, ncc, mc),
        grid=(nq,),
        in_specs=[pl.BlockSpec((None, B, nr, kq), lambda q: (q, 0, 0, 0)),
                  pl.BlockSpec((None, SUBLANES, kq // 4), lambda q: (q, 0, 0))],
        out_specs=pl.BlockSpec((None, B, nr, kq), lambda q: (q, 0, 0, 0)),
        out_shape=jax.ShapeDtypeStruct((nq, B, nr, kq), F32),
        compiler_params=_cparams(("parallel",)),
        name="s5_scan",
    )(xs, aq_q)
    nt = 2
    jt = S5_CHUNK // nt
    y4 = pl.pallas_call(
        functools.partial(_s5_out_kernel, ncc),
        grid=(nq, nt, B),
        in_specs=[pl.BlockSpec((None, None, nr, kq), lambda q, t, b: (q, b, 0, 0)),
                  pl.BlockSpec((None, None, nr, kq), lambda q, t, b: (q, b, 0, 0)),
                  pl.BlockSpec((None, 2, kq, kq // nt), lambda q, t, b: (q, 0, 0, t))],
        out_specs=pl.BlockSpec((None, mc, jt * GRID_W, LANES), lambda q, t, b: (b, 0, t, q)),
        out_shape=jax.ShapeDtypeStruct((B, mc, rows_m, W), F32),
        compiler_params=_cparams(("parallel", "parallel", "parallel")),
        name="s5_out",
    )(z, s, ops_q)
    return y4.reshape(B, L, W)


def _s5_operators(lam_re, lam_im, log_step, b_re, b_im, c_re, c_im):
    hi = lax.Precision.HIGHEST
    Q = S5_CHUNK
    G, P = lam_re.shape[1], lam_re.shape[2]
    H = b_re.shape[-1]
    GL = LANES // H
    NQ = G // GL
    lre = jnp.minimum(lam_re, -1e-4)
    step = jnp.exp(log_step)[..., None]
    den = lre * lre + lam_im * lam_im
    mag = jnp.exp(lre * step)
    ar = mag * jnp.cos(lam_im * step)
    ai = mag * jnp.sin(lam_im * step)
    nr, ni = ar - 1.0, ai
    cr = (nr * lre + ni * lam_im) / den
    ci = (ni * lre - nr * lam_im) / den
    bbr = cr[..., None] * b_re - ci[..., None] * b_im
    bbi = cr[..., None] * b_im + ci[..., None] * b_re
    k = jnp.arange(Q + 1, dtype=F32)[:, None, None, None]
    pmag = jnp.exp(k * (lre * step)[None])
    pr = pmag * jnp.cos(k * (lam_im * step)[None])
    pi = pmag * jnp.sin(k * (lam_im * step)[None])
    car = c_re[None] * pr[:, :, :, None, :] - c_im[None] * pi[:, :, :, None, :]
    cai = c_re[None] * pi[:, :, :, None, :] + c_im[None] * pr[:, :, :, None, :]
    abr = pr[..., None] * bbr[None] - pi[..., None] * bbi[None]
    abi = pr[..., None] * bbi[None] + pi[..., None] * bbr[None]
    kl = (jnp.einsum('kdgop,dgpi->kdgoi', car[:Q], bbr, precision=hi)
          - jnp.einsum('kdgop,dgpi->kdgoi', cai[:Q], bbi, precision=hi))
    tin = jnp.arange(Q)[:, None]
    tout = jnp.arange(Q)[None, :]

    def toep(kd, lag):
        m = kd[jnp.clip(lag, 0, Q - 1)]
        m = jnp.where((lag >= 0)[:, :, None, None, None], m, 0.0)
        return m.transpose(2, 0, 4, 1, 3)

    kt = toep(kl[:, 0], tout - tin) + toep(kl[:, 1], tin - tout)
    kq = Q * LANES
    kc = Q * H
    assert kc == 4 * P, "compact operators share one width"
    k_c = kt.reshape(NQ, GL, Q, H, kc).transpose(0, 2, 1, 3, 4).reshape(NQ, kq, kc)

    def chunk_in(x):
        f = x[:Q, 0][::-1]
        b = x[:Q, 1]
        return jnp.stack([f, b]).transpose(0, 2, 1, 4, 3).reshape(2, NQ, GL, Q, H, P)
    in_c = jnp.stack([chunk_in(abr), chunk_in(abi)], axis=1)
    in_c = in_c.transpose(2, 4, 3, 5, 0, 1, 6).reshape(NQ, kq, kc)

    def chunk_out(x):
        f = x[1:Q + 1, 0]
        b = x[1:Q + 1, 1][::-1]
        return jnp.stack([f, b]).transpose(0, 2, 4, 1, 3).reshape(2, NQ, GL, P, kc)
    out_c = jnp.stack([chunk_out(car), -chunk_out(cai)], axis=1)
    out_c = out_c.transpose(2, 0, 1, 3, 4, 5).reshape(NQ, kq, kc)

    comp = jnp.stack([k_c, out_c, in_c], axis=1).astype(BF16)
    eye = lambda n: jnp.eye(n, dtype=BF16)
    ones = jnp.ones((1, GL), BF16)
    spread = jnp.stack([jnp.kron(eye(Q), jnp.kron(ones, eye(H))),
                        jnp.kron(eye(4), jnp.kron(ones, eye(P)))])
    ops_q = _s5_expand(comp, spread, GL, H.bit_length() - 1, P.bit_length() - 1)
    aq = jnp.stack([pr[Q, 0], pi[Q, 0], pr[Q, 1], pi[Q, 1]], axis=0)
    aq = aq.reshape(4, NQ, GL * P).transpose(1, 0, 2)
    aq = jnp.pad(aq, ((0, 0), (0, SUBLANES - 4), (0, 0)))
    return ops_q, aq


def _outproj_kernel(x_ref, hf_ref, hb_ref, gg_ref, y_ref, us_ref, mod_ref, d_ref, wglu_ref,
                    bglu_ref, wout_ref, gpost_ref, gffn_ref, wq_ref, x1_ref, f_ref, q_ref):
    w = hf_ref.shape[-1]
    rg = (hf_ref[...] + hb_ref[...]) * gg_ref[...]
    z = _gelu(y_ref[...] + d_ref[...] * us_ref[...])
    glu = z * _sigmoid(_dot(z.astype(BF16), wglu_ref[...]) + bglu_ref[...])
    o = _dot(rg.astype(BF16), wout_ref[:w, :]) + _dot(glu.astype(BF16), wout_ref[w:, :])
    x1 = x_ref[...] + mod_ref[2:3, :] * _rms(o, gpost_ref[...])
    x1_ref[...] = x1
    f = (_rms(x1, gffn_ref[...]) * (1.0 + mod_ref[4:5, :]) + mod_ref[3:4, :]).astype(BF16)
    f_ref[...] = f
    q_ref[...] = _dot(f, wq_ref[...]).astype(BF16)


def _outproj(x, hf, hb, lc, gg, y, us, mods, d, wglu_bf, bglu, wout_bf, gpost, gffn, wq_bf):
    B, L, D = x.shape
    W = gg.shape[-1]
    NQ = wq_bf.shape[1]
    tt = 256
    off = lc // tt
    tok = lambda n: pl.BlockSpec((None, tt, n), lambda b, i: (b, i, 0))
    rgs = pl.BlockSpec((None, tt, W), lambda b, i: (b, i + off, 0))
    full = lambda r, c: pl.BlockSpec((r, c), lambda b, i: (0, 0))
    return pl.pallas_call(
        _outproj_kernel,
        grid=(B, L // tt),
        in_specs=[tok(D), rgs, rgs, tok(W), tok(W), tok(W),
                  pl.BlockSpec((None, 8, D), lambda b, i: (b, 0, 0)),
                  full(1, W), full(W, W), full(1, W), full(2 * W, D), full(1, D), full(1, D),
                  full(D, NQ)],
        out_specs=[tok(D), tok(D), tok(NQ)],
        out_shape=[jax.ShapeDtypeStruct((B, L, D), F32),
                   jax.ShapeDtypeStruct((B, L, D), BF16),
                   jax.ShapeDtypeStruct((B, L, NQ), BF16)],
        compiler_params=_cparams(("parallel", "parallel")),
        name="outproj",
    )(x, hf, hb, gg, y, us, mods, d.reshape(1, W), wglu_bf, bglu.reshape(1, W), wout_bf,
      gpost.reshape(1, D), gffn.reshape(1, D), wq_bf)


def _top_values(s, n, with_rank=False):
    vals = []
    for r in range(n):
        m = jnp.max(s, axis=0, keepdims=True)
        vals.append(m)
        if with_rank or r + 1 < n:
            s = jnp.where(s == m, NEG_BIG * (r + 1), s)
    if not with_rank:
        return vals
    rank = jnp.where(s <= NEG_BIG, s * (1.0 / NEG_BIG) - 1.0, float(n))
    return vals, rank


def _peer_kernel(f_ref, q_ref, x1_ref, mod_ref, gpost_ref, gc_ref, keys_ref, u_ref, vt_ref, o_ref,
                 nsel_sc, pak_sc, rb_sc, pb_sc, s_sc, act_sc, acc_sc):
    e = pl.program_id(2)
    ne = pl.num_programs(2)
    nk = rb_sc.shape[1]
    ec = s_sc.shape[0]
    cps = u_ref.shape[0] // ec
    npc = ec // nk
    n_top = PEER_TOPK + 1
    f = f_ref[...]

    @pl.when(e == 0)
    def _():
        acc_sc[...] = jnp.zeros_like(acc_sc)
        hd = q_ref.shape[1] // PEER_HEADS // 2
        for h in range(PEER_HEADS):
            qa = q_ref[:, (2 * h) * hd:(2 * h + 1) * hd]
            qb = q_ref[:, (2 * h + 1) * hd:(2 * h + 2) * hd]
            sa_all = _dot_nt(keys_ref[h, 0], qa)
            sb_all = _dot_nt(keys_ref[h, 1], qb)
            for lt in range(sa_all.shape[1] // LANES):
                cols = slice(lt * LANES, (lt + 1) * LANES)
                sa, sb = sa_all[:, cols], sb_all[:, cols]
                va = _top_values(sa, n_top)
                vb, rank_b = _top_values(sb, n_top, with_rank=True)
                cands = [va[i] + vb[j] for i in range(n_top) for j in range(n_top)
                         if (i + 1) * (j + 1) <= n_top]
                pad = (-len(cands)) % SUBLANES
                cands += [jnp.full_like(va[0], NEG_BIG)] * pad
                best = _top_values(jnp.concatenate(cands, axis=0), n_top)
                z = jnp.ones_like(best[0])
                for v in best[1:PEER_TOPK]:
                    z = z + jnp.exp(v - best[0])
                tmid = 0.5 * (best[PEER_TOPK - 1] + best[PEER_TOPK])
                nsel = jnp.zeros_like(sa)
                for j in range(n_top):
                    nsel = jnp.where(sa > tmid - vb[j], float(j + 1), nsel)
                nsel_sc[h, :, cols] = nsel
                pak_sc[h, :, cols] = jnp.exp(sa - va[0]) * (0.5 / z)
                rb_sc[h, :, cols] = rank_b.astype(BF16)
                pb_sc[h, :, cols] = jnp.exp(sb - vb[0]).astype(BF16)

    tt = f.shape[0]
    tw = min(tt, 2 * LANES)
    gelu_c1 = gc_ref[0:1, 0:1].astype(BF16)
    gelu_c3 = gc_ref[1:2, 0:1].astype(BF16)

    def gate(chunk):
        for j in range(npc):
            ea = chunk * npc + j
            rows = slice(j * nk, (j + 1) * nk)
            for t0 in range(0, tt, tw):
                cols = slice(t0, t0 + tw)
                s = s_sc[rows, cols]
                t = jnp.tanh(s * (gelu_c1 + gelu_c3 * (s * s)))
                g = s + s * t
                wt = jnp.zeros((nk, tw), BF16)
                for h in range(PEER_HEADS):
                    ns = nsel_sc[h, pl.ds(ea, 1), cols].astype(BF16)
                    pa = pak_sc[h, pl.ds(ea, 1), cols].astype(BF16)
                    wt = wt + jnp.where(rb_sc[h, :, cols] < ns, pa * pb_sc[h, :, cols], 0.0)
                act_sc[rows, cols] = g * wt

    for c in range(cps):
        s_sc[...] = _dot_nt(u_ref[c * ec:(c + 1) * ec, :], f).astype(BF16)
        gate(e * cps + c)
        acc_sc[...] += _dot(vt_ref[:, c * ec:(c + 1) * ec], act_sc[...])

    @pl.when(e == ne - 1)
    def _():
        y = acc_sc[...].T
        o_ref[...] = x1_ref[...] + mod_ref[5:6, :] * _rms(y, gpost_ref[...])


def _peer(f, q, x1, mods, gpost, keys_bf, u_bf, vt_bf):
    B, L, D = f.shape
    NQ = q.shape[-1]
    E = u_bf.shape[0]
    nk = keys_bf.shape[2]
    tt = min(512, L)
    ec = min(1024, E // 2)
    cps = 2
    gelu_coef = np.zeros((SUBLANES, LANES), np.float32)
    gelu_coef[0] = 0.7978845608028654
    gelu_coef[1] = 0.7978845608028654 * 0.044715
    gelu_coef = jnp.asarray(gelu_coef)
    tok = lambda n: pl.BlockSpec((None, tt, n), lambda b, i, e: (b, i, 0))
    return pl.pallas_call(
        _peer_kernel,
        grid=(B, L // tt, E // (cps * ec)),
        in_specs=[tok(D), tok(NQ), tok(D),
                  pl.BlockSpec((None, 8, D), lambda b, i, e: (b, 0, 0)),
                  pl.BlockSpec((1, D), lambda b, i, e: (0, 0)),
                  pl.BlockSpec((SUBLANES, LANES), lambda b, i, e: (0, 0)),
                  pl.BlockSpec(keys_bf.shape, lambda b, i, e: (0, 0, 0, 0)),
                  pl.BlockSpec((cps * ec, D), lambda b, i, e: (e, 0)),
                  pl.BlockSpec((D, cps * ec), lambda b, i, e: (0, e))],
        out_specs=tok(D),
        out_shape=jax.ShapeDtypeStruct((B, L, D), F32),
        scratch_shapes=[pltpu.VMEM((PEER_HEADS, nk, tt), F32)] * 2
        + [pltpu.VMEM((PEER_HEADS, nk, tt), BF16)] * 2
        + [pltpu.VMEM((ec, tt), BF16)] * 2 + [pltpu.VMEM((D, tt), F32)],
        compiler_params=_cparams(("parallel", "parallel", "arbitrary")),
        name="peer",
    )(f, q, x1, mods, gpost.reshape(1, D), gelu_coef, keys_bf, u_bf, vt_bf)


def _transpose_cast_kernel(x_ref, o_ref):
    o_ref[...] = x_ref[...].T.astype(o_ref.dtype)


def _transpose_cast(x, dtype):
    R, C = x.shape
    tr = min(512, R)
    return pl.pallas_call(
        _transpose_cast_kernel,
        grid=(R // tr,),
        in_specs=[pl.BlockSpec((tr, C), lambda i: (i, 0))],
        out_specs=pl.BlockSpec((C, tr), lambda i: (0, i)),
        out_shape=jax.ShapeDtypeStruct((C, R), dtype),
        compiler_params=_cparams(("parallel",)),
        name="transpose_cast",
    )(x)


def _block_diag(w):
    H, n, _ = w.shape
    eye = jnp.eye(H, dtype=w.dtype)
    return (eye[:, None, :, None] * w[:, :, None, :]).reshape(H * n, H * n)


def kernel(x, c, ctx, c_ctx, w_mod, b_mod, g_pre_mix, g_post_mix, g_pre_ffn, g_post_ffn, w_in, rg_conv_w, rg_conv_b, rg_wa, rg_ba, rg_wx, rg_bx, rg_lam, s5_lam_re, s5_lam_im, s5_log_step, s5_b_re, s5_b_im, s5_c_re, s5_c_im, s5_d, s5_w_glu, s5_b_glu, w_out, peer_wq, peer_keys, peer_u, peer_v):
    B, L, D = x.shape
    Lc = ctx.shape[1]
    assert w_mod.shape[0] == 1, "single-layer block"
    assert L % 512 == 0 or L == 256
    assert Lc % 256 == 0 and L % (GRID_W * S5_CHUNK) == 0
    W = w_in.shape[2] // 3

    nb8 = -(-(B + 1) // 8) * 8
    cvecs = jnp.concatenate([c, c_ctx[None, :], jnp.zeros((nb8 - B - 1, D), F32)], axis=0)
    mods = _modulation(cvecs, w_mod[0], b_mod[0])
    mods = jnp.pad(mods.reshape(nb8, N_MOD, D), ((0, 0), (0, 8 - N_MOD), (0, 0)))

    w_in_bf = w_in[0].astype(BF16)
    xr_l, gg_l, us_l = _inproj(x, mods, lambda b: b, g_pre_mix[0], w_in_bf)
    xr_c, _, us_c = _inproj(ctx, mods, lambda b: B, g_pre_mix[0], w_in_bf)

    xr_all = jnp.concatenate([xr_c, xr_l], axis=1)
    cw = jnp.pad(rg_conv_w[0], ((0, SUBLANES - RG_CONV), (0, 0)))
    cb = rg_conv_b[0].reshape(1, W)
    hs = []
    for d in range(2):
        wg = jnp.concatenate([_block_diag(rg_wa[0, d]), _block_diag(rg_wx[0, d])], axis=1).astype(BF16)
        bg = jnp.concatenate([rg_ba[0, d], rg_bx[0, d]]).reshape(1, 2 * W)
        hs.append(_rglru_dir(xr_all, Lc, d == 1, cw, cb, wg, bg, rg_lam[0, d].reshape(1, W)))

    ops_q, aq_q = _s5_operators(s5_lam_re[0], s5_lam_im[0], s5_log_step[0], s5_b_re[0],
                                s5_b_im[0], s5_c_re[0], s5_c_im[0])
    y_l = _s5(us_c, us_l, ops_q, aq_q)

    x1, f, q = _outproj(x, hs[0], hs[1], Lc, gg_l, y_l, us_l, mods, s5_d[0],
                        s5_w_glu[0].astype(BF16), s5_b_glu[0], w_out[0].astype(BF16),
                        g_post_mix[0], g_pre_ffn[0], peer_wq[0].astype(BF16))

    return _peer(f, q, x1, mods, g_post_ffn[0], peer_keys[0].astype(BF16),
                 peer_u[0].astype(BF16), _transpose_cast(peer_v[0], BF16))
```

```python
import functools

import jax
import jax.numpy as jnp
import numpy as np
from jax import lax
from jax.experimental import pallas as pl
from jax.experimental.pallas import tpu as pltpu

F32 = jnp.float32
BF16 = jnp.bfloat16

EPS = 1e-6
N_MOD = 6
GRID_W = 64
RG_HEADS = 8
RG_CONV = 4
RG_C = 8.0
S5_GROUP = 16
S5_STATE = 64
S5_CHUNK = 16
PEER_HEADS = 8
PEER_TOPK = 16
NEG_BIG = -(2.0 ** 100)

SUBLANES = 8
LANES = 128
VMEM_LIMIT = 56 * 1024 * 1024


def _cparams(sem):
    return pltpu.CompilerParams(dimension_semantics=sem, vmem_limit_bytes=VMEM_LIMIT)


def _gelu(x):
    return 0.5 * x * (1.0 + jnp.tanh(0.7978845608028654 * (x + 0.044715 * (x * x * x))))


def _sigmoid(x):
    return 1.0 / (1.0 + jnp.exp(-x))


def _rms(x, g):
    return x * lax.rsqrt(jnp.mean(x * x, axis=-1, keepdims=True) + EPS) * g


def _dot(a, b):
    return jnp.dot(a, b, preferred_element_type=F32)


def _dot_nt(a, b):
    return lax.dot_general(a, b, (((1,), (1,)), ((), ())), preferred_element_type=F32)


def _mod_kernel(c_ref, w_ref, b_ref, o_ref):
    c = c_ref[...]
    s = c * _sigmoid(c)
    o_ref[...] = _dot(s.astype(BF16), w_ref[...].astype(BF16)) + b_ref[...]


def _modulation(cvecs, w_mod, b_mod):
    R, D = cvecs.shape
    N = w_mod.shape[1]
    tn = N // N_MOD
    return pl.pallas_call(
        _mod_kernel,
        grid=(N // tn,),
        in_specs=[pl.BlockSpec((R, D), lambda j: (0, 0)),
                  pl.BlockSpec((D, tn), lambda j: (0, j)),
                  pl.BlockSpec((1, tn), lambda j: (0, j))],
        out_specs=pl.BlockSpec((R, tn), lambda j: (0, j)),
        out_shape=jax.ShapeDtypeStruct((R, N), F32),
        compiler_params=_cparams(("arbitrary",)),
        name="modulation",
    )(cvecs, w_mod, b_mod.reshape(1, N))


def _inproj_kernel(x_ref, mod_ref, g_ref, w_ref, xr_ref, gg_ref, us_ref):
    x = x_ref[...]
    h = _rms(x, g_ref[...]) * (1.0 + mod_ref[1:2, :]) + mod_ref[0:1, :]
    p = _dot(h.astype(BF16), w_ref[...])
    w = xr_ref.shape[-1]
    xr_ref[...] = p[:, :w]
    gg_ref[...] = _gelu(p[:, w:2 * w])
    us_ref[...] = p[:, 2 * w:]


def _inproj(x, mods, mod_index, g, w_in_bf):
    B, L, D = x.shape
    W = w_in_bf.shape[1] // 3
    tt = min(512, L)
    out = jax.ShapeDtypeStruct((B, L, W), F32)
    ospec = pl.BlockSpec((None, tt, W), lambda b, i: (b, i, 0))
    return pl.pallas_call(
        _inproj_kernel,
        grid=(B, L // tt),
        in_specs=[pl.BlockSpec((None, tt, D), lambda b, i: (b, i, 0)),
                  pl.BlockSpec((None, 8, D), lambda b, i: (mod_index(b), 0, 0)),
                  pl.BlockSpec((1, D), lambda b, i: (0, 0)),
                  pl.BlockSpec((D, 3 * W), lambda b, i: (0, 0))],
        out_specs=[ospec, ospec, ospec],
        out_shape=[out, out, out],
        compiler_params=_cparams(("parallel", "parallel")),
        name="inproj",
    )(x, mods, g.reshape(1, D), w_in_bf)


def _rglru_kernel(rev, ntc, nt, cur_ref, prev_ref, next_ref, cw_ref, cb_ref, wg_ref, bg_ref,
                  lam_ref, h_ref, carry_sc, a_sc, b_sc, hh_sc, pp_sc):
    i = pl.program_id(1)
    tc, w = cur_ref.shape
    seg = tc // SUBLANES
    if rev:
        t = jnp.where(i < ntc, ntc - 1 - i, nt - 1 - (i - ntc))
    else:
        t = i

    @pl.when(i == 0)
    def _():
        carry_sc[...] = jnp.zeros_like(carry_sc)

    no_prev = jnp.logical_or(t == 0, t == ntc)
    no_next = jnp.logical_or(t == ntc - 1, t == nt - 1)
    cur = cur_ref[...]
    prev = jnp.where(no_prev, 0.0, prev_ref[...])
    nxt = jnp.where(no_next, 0.0, next_ref[...])
    ext = jnp.concatenate([prev, cur, nxt], axis=0)
    left = RG_CONV // 2
    xc = cb_ref[...]
    for k in range(RG_CONV):
        off = SUBLANES - left + k
        xc = xc + ext[off:off + tc, :] * cw_ref[k:k + 1, :]

    gates = _dot(xc.astype(BF16), wg_ref[...]) + bg_ref[...]
    r = _sigmoid(gates[:, :w])
    ig = _sigmoid(gates[:, w:])
    lam = lam_ref[...]
    softplus = jnp.maximum(-lam, 0.0) + jnp.log(1.0 + jnp.exp(-jnp.abs(lam)))
    log_a = -RG_C * r * softplus
    a = jnp.exp(log_a)
    b = jnp.sqrt(jnp.maximum(1.0 - a * a, 0.0)) * (ig * xc)
    pitch = a_sc.shape[1] // SUBLANES
    for j in range(w // LANES):
        cols = slice(j * LANES, (j + 1) * LANES)
        for s in range(SUBLANES):
            a_sc[j, s * pitch:s * pitch + seg, :] = a[s * seg:(s + 1) * seg, cols]
            b_sc[j, s * pitch:s * pitch + seg, :] = b[s * seg:(s + 1) * seg, cols]

    for j in range(w // LANES):
        cols = slice(j * LANES, (j + 1) * LANES)
        hh = jnp.zeros((SUBLANES, LANES), F32)
        pp = jnp.ones((SUBLANES, LANES), F32)
        for k in (range(seg - 1, -1, -1) if rev else range(seg)):
            rows = pl.ds(k, SUBLANES, stride=pitch)
            ak = a_sc[j, rows, :]
            hh = ak * hh + b_sc[j, rows, :]
            pp = pp * ak
            hh_sc[j, rows, :] = hh
            pp_sc[j, rows, :] = pp
        c = carry_sc[0:1, cols]
        cin = [None] * SUBLANES
        for s in (range(SUBLANES - 1, -1, -1) if rev else range(SUBLANES)):
            cin[s] = c
            c = hh[s:s + 1, :] + pp[s:s + 1, :] * c
        carry_sc[:, cols] = jnp.broadcast_to(c, (SUBLANES, LANES))
        for s in range(SUBLANES):
            src = slice(s * pitch, s * pitch + seg)
            h_ref[s * seg:(s + 1) * seg, cols] = hh_sc[j, src, :] + pp_sc[j, src, :] * cin[s]


def _rglru_dir(xr_all, lc, rev, cw, cb, wg_bf, bg, lam):
    B, LT, W = xr_all.shape
    tc = 256
    nt = LT // tc
    ntc = lc // tc
    hb = tc // SUBLANES

    def tile(i):
        if rev:
            return jnp.where(i < ntc, ntc - 1 - i, nt - 1 - (i - ntc))
        return i

    nblk8 = LT // SUBLANES
    kern = functools.partial(_rglru_kernel, rev, ntc, nt)
    return pl.pallas_call(
        kern,
        grid=(B, nt),
        in_specs=[pl.BlockSpec((None, tc, W), lambda b, i: (b, tile(i), 0)),
                  pl.BlockSpec((None, SUBLANES, W),
                               lambda b, i: (b, jnp.maximum(tile(i) * hb - 1, 0), 0)),
                  pl.BlockSpec((None, SUBLANES, W),
                               lambda b, i: (b, jnp.minimum((tile(i) + 1) * hb, nblk8 - 1), 0)),
                  pl.BlockSpec((SUBLANES, W), lambda b, i: (0, 0)),
                  pl.BlockSpec((1, W), lambda b, i: (0, 0)),
                  pl.BlockSpec((W, 2 * W), lambda b, i: (0, 0)),
                  pl.BlockSpec((1, 2 * W), lambda b, i: (0, 0)),
                  pl.BlockSpec((1, W), lambda b, i: (0, 0))],
        out_specs=pl.BlockSpec((None, tc, W), lambda b, i: (b, tile(i), 0)),
        out_shape=jax.ShapeDtypeStruct((B, LT, W), F32),
        scratch_shapes=[pltpu.VMEM((SUBLANES, W), F32)]
        + [pltpu.VMEM((W // LANES, SUBLANES * (tc // SUBLANES + SUBLANES), LANES), F32)] * 4,
        compiler_params=_cparams(("parallel", "arbitrary")),
        name="rglru_bwd" if rev else "rglru_fwd",
    )(xr_all, xr_all, xr_all, cw, cb, wg_bf, bg, lam)


def _s5_chunk_rows(xc_ref, xl_ref):
    ncc = xc_ref.shape[0] // S5_CHUNK
    cols = []
    for j in range(S5_CHUNK):
        parts = [xc_ref[pl.ds(j, ncc, stride=S5_CHUNK), :]]
        parts += [xl_ref[m, j * GRID_W:(j + 1) * GRID_W, :] for m in range(xl_ref.shape[0])]
        cols.append(jnp.concatenate(parts, axis=0))
    return jnp.concatenate(cols, axis=1).astype(BF16)


def _s5_in_kernel(xc_ref, xl_ref, min_ref, z_ref, xs_ref):
    z = _s5_chunk_rows(xc_ref, xl_ref)
    z_ref[...] = z
    xs_ref[...] = _dot(z, min_ref[...])


def _s5_scan_kernel(ncc, mc, xs_ref, aq_ref, s_ref):
    nb, nr, _ = xs_ref.shape
    ns = xs_ref.shape[2] // 4
    ncl = nr - ncc

    def lat_row(c):
        return ncc + (c % mc) * GRID_W + c // mc

    def advance(row, d, re, im):
        lo = 2 * d * ns
        for b in range(nb):
            s_ref[b, pl.ds(row, 1), lo:lo + ns] = re[b:b + 1, :]
            s_ref[b, pl.ds(row, 1), lo + ns:lo + 2 * ns] = im[b:b + 1, :]
        xr = jnp.concatenate([xs_ref[b, pl.ds(row, 1), lo:lo + ns] for b in range(nb)], axis=0)
        xi = jnp.concatenate([xs_ref[b, pl.ds(row, 1), lo + ns:lo + 2 * ns] for b in range(nb)],
                             axis=0)
        aqr = aq_ref[2 * d:2 * d + 1, :]
        aqi = aq_ref[2 * d + 1:2 * d + 2, :]
        return aqr * re - aqi * im + xr, aqr * im + aqi * re + xi

    def body(k, carry):
        fre, fim, bre, bim = carry
        in_ctx = k < ncc
        row_f = jnp.where(in_ctx, k, lat_row(jnp.maximum(k - ncc, 0)))
        row_b = jnp.where(in_ctx, ncc - 1 - k, lat_row(jnp.minimum(ncl - 1 - (k - ncc), ncl - 1)))
        fre, fim = advance(row_f, 0, fre, fim)
        bre, bim = advance(row_b, 1, bre, bim)
        return fre, fim, bre, bim

    z = jnp.zeros((nb, ns), F32)
    lax.fori_loop(0, nr, body, (z, z, z, z))


def _s5_out_kernel(ncc, z_ref, s_ref, km_ref, y_ref):
    yt = _dot(z_ref[ncc:, :], km_ref[0]) + _dot(s_ref[ncc:, :].astype(BF16), km_ref[1])
    for m in range(y_ref.shape[0]):
        for jj in range(y_ref.shape[1] // GRID_W):
            y_ref[m, jj * GRID_W:(jj + 1) * GRID_W, :] = (
                yt[m * GRID_W:(m + 1) * GRID_W, jj * LANES:(jj + 1) * LANES])


S5_OP_K, S5_OP_OUT, S5_OP_IN = 0, 1, 2


def _s5_expand_kernel(gl, hsh, psh, c_ref, e_ref, o_ref):
    which = pl.program_id(1)
    x = _dot(c_ref[...], e_ref[...])
    tr = x.shape[0]
    row = lax.broadcasted_iota(jnp.int32, x.shape, 0) + pl.program_id(2) * tr
    col = lax.broadcasted_iota(jnp.int32, x.shape, 1)
    rsh = jnp.where(which == S5_OP_OUT, psh, hsh)
    csh = jnp.where(which == S5_OP_IN, psh, hsh)
    keep = ((row >> rsh) & (gl - 1)) == ((col >> csh) & (gl - 1))
    o_ref[...] = jnp.where(keep, x, 0.0).astype(BF16)


def _s5_expand(comp, spread, gl, hsh, psh):
    nq, nop, kq, kc = comp.shape
    tr = 512
    return pl.pallas_call(
        functools.partial(_s5_expand_kernel, gl, hsh, psh),
        grid=(nq, nop, kq // tr),
        in_specs=[pl.BlockSpec((None, None, tr, kc), lambda q, w, r: (q, w, r, 0)),
                  pl.BlockSpec((None, kc, kq), lambda q, w, r: (w // S5_OP_IN, 0, 0))],
        out_specs=pl.BlockSpec((None, None, tr, kq), lambda q, w, r: (q, w, r, 0)),
        out_shape=jax.ShapeDtypeStruct((nq, nop, kq, kq), BF16),
        compiler_params=_cparams(("parallel", "parallel", "parallel")),
        name="s5_expand",
    )(comp, spread)


def _s5(us_c, us_l, ops_q, aq_q):
    B, Lc, W = us_c.shape
    L = us_l.shape[1]
    nq = W // LANES
    ncc, ncl = Lc // S5_CHUNK, L // S5_CHUNK
    nr = ncc + ncl
    mc = L // (GRID_W * S5_CHUNK)
    kq = S5_CHUNK * LANES
    rows_m = S5_CHUNK * GRID_W
    us_l4 = us_l.reshape(B, mc, rows_m, W)
    z, xs = pl.pallas_call(
        _s5_in_kernel,
        grid=(nq, B),
        in_specs=[pl.BlockSpec((None, Lc, LANES), lambda q, b: (b, 0, q)),
                  pl.BlockSpec((None, mc, rows_m, LANES), lambda q, b: (b, 0, 0, q)),
                  pl.BlockSpec((None, None, kq, kq), lambda q, b: (q, S5_OP_IN, 0, 0))],
        out_specs=[pl.BlockSpec((None, None, nr, kq), lambda q, b: (q, b, 0, 0))] * 2,
        out_shape=[jax.ShapeDtypeStruct((nq, B, nr, kq), BF16),
                   jax.ShapeDtypeStruct((nq, B, nr, kq), F32)],
        compiler_params=_cparams(("parallel", "parallel")),
        name="s5_in",
    )(us_c, us_l4, ops_q)
    s = pl.pallas_call(
        functools.partial(_s5_scan_kernel, ncc, mc),
        grid=(nq,),
        in_specs=[pl.BlockSpec((None, B, nr, kq), lambda q: (q, 0, 0, 0)),
                  pl.BlockSpec((None, SUBLANES, kq // 4), lambda q: (q, 0, 0))],
        out_specs=pl.BlockSpec((None, B, nr, kq), lambda q: (q, 0, 0, 0)),
        out_shape=jax.ShapeDtypeStruct((nq, B, nr, kq), F32),
        compiler_params=_cparams(("parallel",)),
        name="s5_scan",
    )(xs, aq_q)
    nt = 2
    jt = S5_CHUNK // nt
    y4 = pl.pallas_call(
        functools.partial(_s5_out_kernel, ncc),
        grid=(nq, nt, B),
        in_specs=[pl.BlockSpec((None, None, nr, kq), lambda q, t, b: (q, b, 0, 0)),
                  pl.BlockSpec((None, None, nr, kq), lambda q, t, b: (q, b, 0, 0)),
                  pl.BlockSpec((None, 2, kq, kq // nt), lambda q, t, b: (q, 0, 0, t))],
        out_specs=pl.BlockSpec((None, mc, jt * GRID_W, LANES), lambda q, t, b: (b, 0, t, q)),
        out_shape=jax.ShapeDtypeStruct((B, mc, rows_m, W), F32),
        compiler_params=_cparams(("parallel", "parallel", "parallel")),
        name="s5_out",
    )(z, s, ops_q)
    return y4.reshape(B, L, W)


def _s5_operators(lam_re, lam_im, log_step, b_re, b_im, c_re, c_im):
    hi = lax.Precision.HIGHEST
    Q = S5_CHUNK
    G, P = lam_re.shape[1], lam_re.shape[2]
    H = b_re.shape[-1]
    GL = LANES // H
    NQ = G // GL
    lre = jnp.minimum(lam_re, -1e-4)
    step = jnp.exp(log_step)[..., None]
    den = lre * lre + lam_im * lam_im
    mag = jnp.exp(lre * step)
    ar = mag * jnp.cos(lam_im * step)
    ai = mag * jnp.sin(lam_im * step)
    nr, ni = ar - 1.0, ai
    cr = (nr * lre + ni * lam_im) / den
    ci = (ni * lre - nr * lam_im) / den
    bbr = cr[..., None] * b_re - ci[..., None] * b_im
    bbi = cr[..., None] * b_im + ci[..., None] * b_re
    k = jnp.arange(Q + 1, dtype=F32)[:, None, None, None]
    pmag = jnp.exp(k * (lre * step)[None])
    pr = pmag * jnp.cos(k * (lam_im * step)[None])
    pi = pmag * jnp.sin(k * (lam_im * step)[None])
    car = c_re[None] * pr[:, :, :, None, :] - c_im[None] * pi[:, :, :, None, :]
    cai = c_re[None] * pi[:, :, :, None, :] + c_im[None] * pr[:, :, :, None, :]
    abr = pr[..., None] * bbr[None] - pi[..., None] * bbi[None]
    abi = pr[..., None] * bbi[None] + pi[..., None] * bbr[None]
    kq = Q * LANES
    kc = Q * H
    assert kc == 4 * P, "compact operators share one width"

    def lag_table(d):
        rr = car[:Q, d].transpose(1, 3, 0, 2).reshape(G, P, kc)
        ri = cai[:Q, d].transpose(1, 3, 0, 2).reshape(G, P, kc)
        return (jnp.einsum('gph,gpc->ghc', bbr[d], rr, precision=hi)
                - jnp.einsum('gph,gpc->ghc', bbi[d], ri, precision=hi))

    shift = (Q - 1) * H
    zf = jnp.pad(lag_table(0), ((0, 0), (0, 0), (shift, 0)))
    zb = lag_table(1).reshape(G, H, Q, H)[:, :, ::-1].reshape(G, H, kc)
    zb = jnp.pad(zb, ((0, 0), (0, 0), (0, shift)))
    kt = jnp.stack([zf[:, :, shift - t * H:shift - t * H + kc]
                    + zb[:, :, shift - t * H:shift - t * H + kc] for t in range(Q)], axis=1)
    k_c = kt.reshape(NQ, GL, Q, H, kc).transpose(0, 2, 1, 3, 4).reshape(NQ, kq, kc)

    def chunk_in(x):
        f = x[:Q, 0][::-1]
        b = x[:Q, 1]
        return jnp.stack([f, b]).transpose(0, 2, 1, 4, 3).reshape(2, NQ, GL, Q, H, P)
    in_c = jnp.stack([chunk_in(abr), chunk_in(abi)], axis=1)
    in_c = in_c.transpose(2, 4, 3, 5, 0, 1, 6).reshape(NQ, kq, kc)

    def chunk_out(x):
        f = x[1:Q + 1, 0]
        b = x[1:Q + 1, 1][::-1]
        return jnp.stack([f, b]).transpose(0, 2, 4, 1, 3).reshape(2, NQ, GL, P, kc)
    out_c = jnp.stack([chunk_out(car), -chunk_out(cai)], axis=1)
    out_c = out_c.transpose(2, 0, 1, 3, 4, 5).reshape(NQ, kq, kc)

    comp = jnp.stack([k_c, out_c, in_c], axis=1).astype(BF16)
    eye = lambda n: jnp.eye(n, dtype=BF16)
    ones = jnp.ones((1, GL), BF16)
    spread = jnp.stack([jnp.kron(eye(Q), jnp.kron(ones, eye(H))),
                        jnp.kron(eye(4), jnp.kron(ones, eye(P)))])
    ops_q = _s5_expand(comp, spread, GL, H.bit_length() - 1, P.bit_length() - 1)
    aq = jnp.stack([pr[Q, 0], pi[Q, 0], pr[Q, 1], pi[Q, 1]], axis=0)
    aq = aq.reshape(4, NQ, GL * P).transpose(1, 0, 2)
    aq = jnp.pad(aq, ((0, 0), (0, SUBLANES - 4), (0, 0)))
    return ops_q, aq


def _outproj_kernel(x_ref, hf_ref, hb_ref, gg_ref, y_ref, us_ref, mod_ref, d_ref, wglu_ref,
                    bglu_ref, wout_ref, gpost_ref, gffn_ref, wq_ref, x1_ref, f_ref, q_ref):
    w = hf_ref.shape[-1]
    rg = (hf_ref[...] + hb_ref[...]) * gg_ref[...]
    z = _gelu(y_ref[...] + d_ref[...] * us_ref[...])
    glu = z * _sigmoid(_dot(z.astype(BF16), wglu_ref[...]) + bglu_ref[...])
    o = _dot(rg.astype(BF16), wout_ref[:w, :]) + _dot(glu.astype(BF16), wout_ref[w:, :])
    x1 = x_ref[...] + mod_ref[2:3, :] * _rms(o, gpost_ref[...])
    x1_ref[...] = x1
    f = (_rms(x1, gffn_ref[...]) * (1.0 + mod_ref[4:5, :]) + mod_ref[3:4, :]).astype(BF16)
    f_ref[...] = f
    q_ref[...] = _dot(f, wq_ref[...]).astype(BF16)


def _outproj(x, hf, hb, lc, gg, y, us, mods, d, wglu_bf, bglu, wout_bf, gpost, gffn, wq_bf):
    B, L, D = x.shape
    W = gg.shape[-1]
    NQ = wq_bf.shape[1]
    tt = 256
    off = lc // tt
    tok = lambda n: pl.BlockSpec((None, tt, n), lambda b, i: (b, i, 0))
    rgs = pl.BlockSpec((None, tt, W), lambda b, i: (b, i + off, 0))
    full = lambda r, c: pl.BlockSpec((r, c), lambda b, i: (0, 0))
    return pl.pallas_call(
        _outproj_kernel,
        grid=(B, L // tt),
        in_specs=[tok(D), rgs, rgs, tok(W), tok(W), tok(W),
                  pl.BlockSpec((None, 8, D), lambda b, i: (b, 0, 0)),
                  full(1, W), full(W, W), full(1, W), full(2 * W, D), full(1, D), full(1, D),
                  full(D, NQ)],
        out_specs=[tok(D), tok(D), tok(NQ)],
        out_shape=[jax.ShapeDtypeStruct((B, L, D), F32),
                   jax.ShapeDtypeStruct((B, L, D), BF16),
                   jax.ShapeDtypeStruct((B, L, NQ), BF16)],
        compiler_params=_cparams(("parallel", "parallel")),
        name="outproj",
    )(x, hf, hb, gg, y, us, mods, d.reshape(1, W), wglu_bf, bglu.reshape(1, W), wout_bf,
      gpost.reshape(1, D), gffn.reshape(1, D), wq_bf)


def _top_values(s, n, with_rank=False):
    vals = []
    for r in range(n):
        m = jnp.max(s, axis=0, keepdims=True)
        vals.append(m)
        if with_rank or r + 1 < n:
            s = jnp.where(s == m, NEG_BIG * (r + 1), s)
    if not with_rank:
        return vals
    rank = jnp.where(s <= NEG_BIG, s * (1.0 / NEG_BIG) - 1.0, float(n))
    return vals, rank


def _peer_kernel(f_ref, q_ref, x1_ref, mod_ref, gpost_ref, gc_ref, keys_ref, u_ref, vt_ref, o_ref,
                 nsel_sc, pak_sc, rb_sc, pb_sc, s_sc, act_sc, acc_sc):
    e = pl.program_id(2)
    ne = pl.num_programs(2)
    nk = rb_sc.shape[1]
    ec = s_sc.shape[0]
    cps = u_ref.shape[0] // ec
    npc = ec // nk
    n_top = PEER_TOPK + 1
    f = f_ref[...]

    @pl.when(e == 0)
    def _():
        acc_sc[...] = jnp.zeros_like(acc_sc)
        hd = q_ref.shape[1] // PEER_HEADS // 2
        for h in range(PEER_HEADS):
            qa = q_ref[:, (2 * h) * hd:(2 * h + 1) * hd]
            qb = q_ref[:, (2 * h + 1) * hd:(2 * h + 2) * hd]
            sa_all = _dot_nt(keys_ref[h, 0], qa)
            sb_all = _dot_nt(keys_ref[h, 1], qb)
            for lt in range(sa_all.shape[1] // LANES):
                cols = slice(lt * LANES, (lt + 1) * LANES)
                sa, sb = sa_all[:, cols], sb_all[:, cols]
                va = _top_values(sa, n_top)
                vb, rank_b = _top_values(sb, n_top, with_rank=True)
                cands = [va[i] + vb[j] for i in range(n_top) for j in range(n_top)
                         if (i + 1) * (j + 1) <= n_top]
                pad = (-len(cands)) % SUBLANES
                cands += [jnp.full_like(va[0], NEG_BIG)] * pad
                best = _top_values(jnp.concatenate(cands, axis=0), n_top)
                z = jnp.ones_like(best[0])
                for v in best[1:PEER_TOPK]:
                    z = z + jnp.exp(v - best[0])
                tmid = 0.5 * (best[PEER_TOPK - 1] + best[PEER_TOPK])
                nsel = jnp.zeros_like(sa)
                for j in range(n_top):
                    nsel = jnp.where(sa > tmid - vb[j], float(j + 1), nsel)
                nsel_sc[h, :, cols] = nsel
                pak_sc[h, :, cols] = jnp.exp(sa - va[0]) * (0.5 / z)
                rb_sc[h, :, cols] = rank_b.astype(BF16)
                pb_sc[h, :, cols] = jnp.exp(sb - vb[0]).astype(BF16)

    tt = f.shape[0]
    tw = min(tt, 2 * LANES)
    gelu_c1 = gc_ref[0:1, 0:1].astype(BF16)
    gelu_c3 = gc_ref[1:2, 0:1].astype(BF16)

    def gate(chunk):
        for j in range(npc):
            ea = chunk * npc + j
            rows = slice(j * nk, (j + 1) * nk)
            for t0 in range(0, tt, tw):
                cols = slice(t0, t0 + tw)
                s = s_sc[rows, cols]
                t = jnp.tanh(s * (gelu_c1 + gelu_c3 * (s * s)))
                g = s + s * t
                wt = jnp.zeros((nk, tw), BF16)
                for h in range(PEER_HEADS):
                    ns = nsel_sc[h, pl.ds(ea, 1), cols].astype(BF16)
                    pa = pak_sc[h, pl.ds(ea, 1), cols].astype(BF16)
                    wt = wt + jnp.where(rb_sc[h, :, cols] < ns, pa * pb_sc[h, :, cols], 0.0)
                act_sc[rows, cols] = g * wt

    for c in range(cps):
        s_sc[...] = _dot_nt(u_ref[c * ec:(c + 1) * ec, :], f).astype(BF16)
        gate(e * cps + c)
        acc_sc[...] += _dot(vt_ref[:, c * ec:(c + 1) * ec], act_sc[...])

    @pl.when(e == ne - 1)
    def _():
        y = acc_sc[...].T
        o_ref[...] = x1_ref[...] + mod_ref[5:6, :] * _rms(y, gpost_ref[...])


def _peer(f, q, x1, mods, gpost, keys_bf, u_bf, vt_bf):
    B, L, D = f.shape
    NQ = q.shape[-1]
    E = u_bf.shape[0]
    nk = keys_bf.shape[2]
    tt = min(512, L)
    ec = min(1024, E // 2)
    cps = 2
    gelu_coef = np.zeros((SUBLANES, LANES), np.float32)
    gelu_coef[0] = 0.7978845608028654
    gelu_coef[1] = 0.7978845608028654 * 0.044715
    gelu_coef = jnp.asarray(gelu_coef)
    tok = lambda n: pl.BlockSpec((None, tt, n), lambda b, i, e: (b, i, 0))
    return pl.pallas_call(
        _peer_kernel,
        grid=(B, L // tt, E // (cps * ec)),
        in_specs=[tok(D), tok(NQ), tok(D),
                  pl.BlockSpec((None, 8, D), lambda b, i, e: (b, 0, 0)),
                  pl.BlockSpec((1, D), lambda b, i, e: (0, 0)),
                  pl.BlockSpec((SUBLANES, LANES), lambda b, i, e: (0, 0)),
                  pl.BlockSpec(keys_bf.shape, lambda b, i, e: (0, 0, 0, 0)),
                  pl.BlockSpec((cps * ec, D), lambda b, i, e: (e, 0)),
                  pl.BlockSpec((D, cps * ec), lambda b, i, e: (0, e))],
        out_specs=tok(D),
        out_shape=jax.ShapeDtypeStruct((B, L, D), F32),
        scratch_shapes=[pltpu.VMEM((PEER_HEADS, nk, tt), F32)] * 2
        + [pltpu.VMEM((PEER_HEADS, nk, tt), BF16)] * 2
        + [pltpu.VMEM((ec, tt), BF16)] * 2 + [pltpu.VMEM((D, tt), F32)],
        compiler_params=_cparams(("parallel", "parallel", "arbitrary")),
        name="peer",
    )(f, q, x1, mods, gpost.reshape(1, D), gelu_coef, keys_bf, u_bf, vt_bf)


def _transpose_cast_kernel(x_ref, o_ref):
    o_ref[...] = x_ref[...].T.astype(o_ref.dtype)


def _transpose_cast(x, dtype):
    R, C = x.shape
    tr = min(512, R)
    return pl.pallas_call(
        _transpose_cast_kernel,
        grid=(R // tr,),
        in_specs=[pl.BlockSpec((tr, C), lambda i: (i, 0))],
        out_specs=pl.BlockSpec((C, tr), lambda i: (0, i)),
        out_shape=jax.ShapeDtypeStruct((C, R), dtype),
        compiler_params=_cparams(("parallel",)),
        name="transpose_cast",
    )(x)


def _block_diag(w):
    H, n, _ = w.shape
    eye = jnp.eye(H, dtype=w.dtype)
    return (eye[:, None, :, None] * w[:, :, None, :]).reshape(H * n, H * n)


def kernel(x, c, ctx, c_ctx, w_mod, b_mod, g_pre_mix, g_post_mix, g_pre_ffn, g_post_ffn, w_in, rg_conv_w, rg_conv_b, rg_wa, rg_ba, rg_wx, rg_bx, rg_lam, s5_lam_re, s5_lam_im, s5_log_step, s5_b_re, s5_b_im, s5_c_re, s5_c_im, s5_d, s5_w_glu, s5_b_glu, w_out, peer_wq, peer_keys, peer_u, peer_v):
    B, L, D = x.shape
    Lc = ctx.shape[1]
    assert w_mod.shape[0] == 1, "single-layer block"
    assert L % 512 == 0 or L == 256
    assert Lc % 256 == 0 and L % (GRID_W * S5_CHUNK) == 0
    W = w_in.shape[2] // 3

    nb8 = -(-(B + 1) // 8) * 8
    cvecs = jnp.concatenate([c, c_ctx[None, :], jnp.zeros((nb8 - B - 1, D), F32)], axis=0)
    mods = _modulation(cvecs, w_mod[0], b_mod[0])
    mods = jnp.pad(mods.reshape(nb8, N_MOD, D), ((0, 0), (0, 8 - N_MOD), (0, 0)))

    w_in_bf = w_in[0].astype(BF16)
    xr_l, gg_l, us_l = _inproj(x, mods, lambda b: b, g_pre_mix[0], w_in_bf)
    xr_c, _, us_c = _inproj(ctx, mods, lambda b: B, g_pre_mix[0], w_in_bf)

    xr_all = jnp.concatenate([xr_c, xr_l], axis=1)
    cw = jnp.pad(rg_conv_w[0], ((0, SUBLANES - RG_CONV), (0, 0)))
    cb = rg_conv_b[0].reshape(1, W)
    hs = []
    for d in range(2):
        wg = jnp.concatenate([_block_diag(rg_wa[0, d]), _block_diag(rg_wx[0, d])], axis=1).astype(BF16)
        bg = jnp.concatenate([rg_ba[0, d], rg_bx[0, d]]).reshape(1, 2 * W)
        hs.append(_rglru_dir(xr_all, Lc, d == 1, cw, cb, wg, bg, rg_lam[0, d].reshape(1, W)))

    ops_q, aq_q = _s5_operators(s5_lam_re[0], s5_lam_im[0], s5_log_step[0], s5_b_re[0],
                                s5_b_im[0], s5_c_re[0], s5_c_im[0])
    y_l = _s5(us_c, us_l, ops_q, aq_q)

    x1, f, q = _outproj(x, hs[0], hs[1], Lc, gg_l, y_l, us_l, mods, s5_d[0],
                        s5_w_glu[0].astype(BF16), s5_b_glu[0], w_out[0].astype(BF16),
                        g_post_mix[0], g_pre_ffn[0], peer_wq[0].astype(BF16))

    return _peer(f, q, x1, mods, g_post_ffn[0], peer_keys[0].astype(BF16),
                 peer_u[0].astype(BF16), _transpose_cast(peer_v[0], BF16))
```
